```python
import math
import jax, jax.numpy as jnp
from jax import lax
import numpy as np

D_MODEL = 2048
BATCH = 4
SEQ = 2048
DEPTH = 1
DEC_BATCH = 32
DEC_SEQ = 1
PAST_LEN = 8192
PAGE_SIZE = 128

D_CONV = D_MODEL // 2
CONV_WIDTH = 31
D_ATTN = D_MODEL // 2
N_HEADS = 8
HEAD_DIM = D_ATTN // N_HEADS // 2
V_DIM = 2 * HEAD_DIM
Q_BLOCK = 128
N_EXPERTS = 32
TOP_K = 4
D_EXPERT = D_MODEL
SWIGLU_ALPHA = 1.702
SWIGLU_LIMIT = 7.0
MOE_BLOCK = 128
RMS_EPS = 1e-5
LN_EPS = 1e-5
D_IN_PROJ = 2 * D_CONV + 3 * D_ATTN + 2 * D_MODEL
SPLIT_POINTS = (D_CONV, 2 * D_CONV, 2 * D_CONV + D_ATTN, 2 * D_CONV + 2 * D_ATTN,
                2 * D_CONV + 3 * D_ATTN, 2 * D_CONV + 3 * D_ATTN + D_MODEL)

kernel_name = "gated_conformer_diffattn_moe_step"


def rmsnorm(x, g):
    xf = x.astype(jnp.float32)
    y = xf * lax.rsqrt(jnp.mean(xf * xf, axis=-1, keepdims=True) + RMS_EPS)
    return (y * g.astype(jnp.float32)).astype(x.dtype)


def layernorm(x, g, b):
    xf = x.astype(jnp.float32)
    mu = jnp.mean(xf, axis=-1, keepdims=True)
    var = jnp.mean(jnp.square(xf - mu), axis=-1, keepdims=True)
    y = (xf - mu) * lax.rsqrt(var + LN_EPS) * g.astype(jnp.float32) + b.astype(jnp.float32)
    return y.astype(x.dtype)


def alibi_slopes():
    return 2.0 ** (-8.0 * jnp.arange(1, N_HEADS + 1, dtype=jnp.float32) / N_HEADS)


def diff_attn_core(q, k, v, q_pos, k_pos, lam):
    s = jnp.einsum('bqhd,bkhd->bhqk', q, k).astype(jnp.float32) * (HEAD_DIM ** -0.5)
    slopes = jnp.repeat(alibi_slopes(), 2)
    dist = (q_pos[:, None] - k_pos[None, :]).astype(jnp.float32)
    s = s - slopes[:, None, None] * dist
    s = jnp.where(dist >= 0, s, -jnp.inf)
    p = jax.nn.softmax(s, axis=-1)
    b_, _, nq, nk = p.shape
    p = p.reshape(b_, N_HEADS, 2, nq, nk)
    a = p[:, :, 0] - lam * p[:, :, 1]
    return jnp.einsum('bhqk,bkhe->bqhe', a.astype(v.dtype), v)


def conv_module(u_ext, w_dw, b_dw, conv_ln_g, conv_ln_b, w_conv_out, b_conv_out):
    y = lax.conv_general_dilated(u_ext, w_dw[:, None, :], window_strides=(1,), padding='VALID',
                                 dimension_numbers=('NWC', 'WIO', 'NWC'),
                                 feature_group_count=D_CONV) + b_dw
    y = jax.nn.silu(layernorm(y, conv_ln_g, conv_ln_b))
    return y @ w_conv_out + b_conv_out


def moe(x, w_router, b_router, w_exp_gate, b_exp_gate, w_exp_up, b_exp_up, w_exp_down, b_exp_down):
    t = x.shape[0]
    logits = (x @ w_router + b_router).astype(jnp.float32)
    top_v, top_i = lax.top_k(logits, TOP_K)
    top_w = jax.nn.softmax(top_v, axis=-1)
    flat_e = top_i.reshape(-1).astype(jnp.int32)
    flat_w = top_w.reshape(-1)
    n_assign = t * TOP_K
    order = jnp.argsort(flat_e).astype(jnp.int32)
    sorted_e = flat_e[order]
    counts = jnp.bincount(flat_e, length=N_EXPERTS).astype(jnp.int32)
    padded = (counts + MOE_BLOCK - 1) // MOE_BLOCK * MOE_BLOCK
    start = jnp.cumsum(counts) - counts
    pend = jnp.cumsum(padded)
    pstart = pend - padded
    dest = pstart[sorted_e] + (jnp.arange(n_assign, dtype=jnp.int32) - start[sorted_e])
    n_rows = -(-n_assign // MOE_BLOCK) * MOE_BLOCK + N_EXPERTS * MOE_BLOCK
    n_blocks = n_rows // MOE_BLOCK
    row_tok = jnp.full((n_rows,), t, jnp.int32).at[dest].set(order // TOP_K)
    row_w = jnp.zeros((n_rows,), jnp.float32).at[dest].set(flat_w[order])
    blk_e = jnp.minimum(jnp.searchsorted(pend, jnp.arange(n_blocks, dtype=jnp.int32) * MOE_BLOCK,
                                         side='right'), N_EXPERTS - 1)
    x_pad = jnp.concatenate([x, jnp.zeros((1, x.shape[1]), x.dtype)], axis=0)
    xb = x_pad[row_tok].reshape(n_blocks, MOE_BLOCK, x.shape[1])

    def expert_block(args):
        xe, e = args
        gt = jnp.minimum(xe @ w_exp_gate[e] + b_exp_gate[e], SWIGLU_LIMIT)
        up = jnp.clip(xe @ w_exp_up[e] + b_exp_up[e], -SWIGLU_LIMIT, SWIGLU_LIMIT)
        hid = gt * jax.nn.sigmoid(SWIGLU_ALPHA * gt) * (up + 1)
        return hid @ w_exp_down[e] + b_exp_down[e]

    yb = lax.map(expert_block, (xb, blk_e)).reshape(n_rows, x.shape[1])
    yb = yb * row_w[:, None].astype(yb.dtype)
    return jnp.zeros_like(x_pad).at[row_tok].add(yb)[:t]


def decoder_layer(x, conv_prev, k_past, v_past, lam_init,
                  norm1_g, w_in, w_dw, b_dw, conv_ln_g, conv_ln_b, w_conv_out, b_conv_out,
                  lambda_q1, lambda_k1, lambda_q2, lambda_k2, subln_g, w_attn_out, w_o,
                  norm2_g, w_router, b_router, w_exp_gate, b_exp_gate, w_exp_up, b_exp_up,
                  w_exp_down, b_exp_down):
    b_, l_, _ = x.shape
    h = rmsnorm(x, norm1_g)
    z = h @ w_in
    glu_a, glu_g, q, k, v, gate_c, gate_a = jnp.split(z, SPLIT_POINTS, axis=-1)

    u = glu_a * jax.nn.sigmoid(glu_g)
    u_ext = jnp.concatenate([conv_prev.astype(u.dtype), u], axis=1)
    conv_new = u_ext[:, -(CONV_WIDTH - 1):]
    conv_out = conv_module(u_ext, w_dw, b_dw, conv_ln_g, conv_ln_b, w_conv_out, b_conv_out)

    q = q.reshape(b_, l_, 2 * N_HEADS, HEAD_DIM)
    k = k.reshape(b_, l_, 2 * N_HEADS, HEAD_DIM)
    v = v.reshape(b_, l_, N_HEADS, V_DIM)
    lq1, lk1 = lambda_q1.astype(jnp.float32), lambda_k1.astype(jnp.float32)
    lq2, lk2 = lambda_q2.astype(jnp.float32), lambda_k2.astype(jnp.float32)
    lam = jnp.exp(jnp.sum(lq1 * lk1)) - jnp.exp(jnp.sum(lq2 * lk2)) + lam_init
    if k_past is None:
        nqb = l_ // Q_BLOCK
        qb = q.reshape(b_, nqb, Q_BLOCK, 2 * N_HEADS, HEAD_DIM).transpose(1, 0, 2, 3, 4)
        pos_b = jnp.arange(l_, dtype=jnp.int32).reshape(nqb, Q_BLOCK)
        k_pos = jnp.arange(l_, dtype=jnp.int32)
        ob = lax.map(lambda a: diff_attn_core(a[0], k, v, a[1], k_pos, lam), (qb, pos_b))
        o = ob.transpose(1, 0, 2, 3, 4).reshape(b_, l_, N_HEADS, V_DIM)
    else:
        past = k_past.shape[1]
        k_all = jnp.concatenate([k_past.astype(k.dtype), k], axis=1)
        v_all = jnp.concatenate([v_past.astype(v.dtype), v], axis=1)
        q_pos = past + jnp.arange(l_, dtype=jnp.int32)
        k_pos = jnp.arange(past + l_, dtype=jnp.int32)
        o = diff_attn_core(q, k_all, v_all, q_pos, k_pos, lam)
    o = rmsnorm(o, subln_g) * (1.0 - lam_init)
    attn_out = o.reshape(b_, l_, D_ATTN) @ w_attn_out

    mixed = jax.nn.sigmoid(gate_c) * conv_out + jax.nn.sigmoid(gate_a) * attn_out
    x = x + mixed @ w_o

    h2 = rmsnorm(x, norm2_g).reshape(b_ * l_, D_MODEL)
    x = x + moe(h2, w_router, b_router, w_exp_gate, b_exp_gate, w_exp_up, b_exp_up,
                w_exp_down, b_exp_down).reshape(b_, l_, D_MODEL)
    return x, k, v, conv_new


def setup_inputs(seed: int = 0) -> dict:
    key = jax.random.key(seed)
    ks = jax.random.split(key, 40)
    n_pages = PAST_LEN // PAGE_SIZE
    n_pool = (5 * DEC_BATCH * n_pages + 3) // 4
    f32 = jnp.float32

    def nrm(k, shape, scale):
        return jax.random.normal(k, shape, f32) * scale

    def gain(k, shape):
        return 1.0 + 0.02 * jax.random.normal(k, shape, f32)

    page_table = jax.random.permutation(ks[5], n_pool)[:DEC_BATCH * n_pages]
    page_table = page_table.reshape(DEC_BATCH, n_pages).astype(jnp.int32)
    return {
        'x_prompt': nrm(ks[0], (BATCH, SEQ, D_MODEL), 1.0),
        'x_sample': nrm(ks[1], (DEC_BATCH, DEC_SEQ, D_MODEL), 1.0),
        'cache_k': nrm(ks[2], (DEPTH, n_pool, PAGE_SIZE, 2 * N_HEADS, HEAD_DIM), 1.0),
        'cache_v': nrm(ks[3], (DEPTH, n_pool, PAGE_SIZE, N_HEADS, V_DIM), 1.0),
        'state_conv': nrm(ks[4], (DEPTH, DEC_BATCH, CONV_WIDTH - 1, D_CONV), 1.0),
        'page_table': page_table,
        'norm1_g': gain(ks[6], (DEPTH, D_MODEL)),
        'w_in': nrm(ks[7], (DEPTH, D_MODEL, D_IN_PROJ), D_MODEL ** -0.5),
        'w_dw': nrm(ks[8], (DEPTH, CONV_WIDTH, D_CONV), CONV_WIDTH ** -0.5),
        'b_dw': nrm(ks[9], (DEPTH, D_CONV), 0.02),
        'conv_ln_g': gain(ks[10], (DEPTH, D_CONV)),
        'conv_ln_b': nrm(ks[11], (DEPTH, D_CONV), 0.02),
        'w_conv_out': nrm(ks[12], (DEPTH, D_CONV, D_MODEL), D_CONV ** -0.5),
        'b_conv_out': nrm(ks[13], (DEPTH, D_MODEL), 0.02),
        'lambda_q1': nrm(ks[14], (DEPTH, HEAD_DIM), 0.1),
        'lambda_k1': nrm(ks[15], (DEPTH, HEAD_DIM), 0.1),
        'lambda_q2': nrm(ks[16], (DEPTH, HEAD_DIM), 0.1),
        'lambda_k2': nrm(ks[17], (DEPTH, HEAD_DIM), 0.1),
        'subln_g': gain(ks[18], (DEPTH, V_DIM)),
        'w_attn_out': nrm(ks[19], (DEPTH, D_ATTN, D_MODEL), D_ATTN ** -0.5),
        'w_o': nrm(ks[20], (DEPTH, D_MODEL, D_MODEL), D_MODEL ** -0.5),
        'norm2_g': gain(ks[21], (DEPTH, D_MODEL)),
        'w_router': nrm(ks[22], (DEPTH, D_MODEL, N_EXPERTS), D_MODEL ** -0.5),
        'b_router': nrm(ks[23], (DEPTH, N_EXPERTS), 0.01),
        'w_exp_gate': nrm(ks[24], (DEPTH, N_EXPERTS, D_MODEL, D_EXPERT), D_MODEL ** -0.5),
        'b_exp_gate': nrm(ks[25], (DEPTH, N_EXPERTS, D_EXPERT), 0.02),
        'w_exp_up': nrm(ks[26], (DEPTH, N_EXPERTS, D_MODEL, D_EXPERT), D_MODEL ** -0.5),
        'b_exp_up': nrm(ks[27], (DEPTH, N_EXPERTS, D_EXPERT), 0.02),
        'w_exp_down': nrm(ks[28], (DEPTH, N_EXPERTS, D_EXPERT, D_MODEL), D_EXPERT ** -0.5),
        'b_exp_down': nrm(ks[29], (DEPTH, N_EXPERTS, D_MODEL), 0.02),
        'final_norm_g': gain(ks[30], (D_MODEL,)),
    }


def reference(x_prompt, x_sample, cache_k, cache_v, state_conv, page_table,
              norm1_g, w_in, w_dw, b_dw, conv_ln_g, conv_ln_b, w_conv_out, b_conv_out,
              lambda_q1, lambda_k1, lambda_q2, lambda_k2, subln_g, w_attn_out, w_o,
              norm2_g, w_router, b_router, w_exp_gate, b_exp_gate, w_exp_up, b_exp_up,
              w_exp_down, b_exp_down, final_norm_g):
    dec_b = x_sample.shape[0]
    yp, ys = x_prompt, x_sample
    kp_l, vp_l, cp_l, ks_l, vs_l, cs_l = [], [], [], [], [], []
    for l in range(DEPTH):
        lam_init = 0.8 - 0.6 * math.exp(-0.3 * l)
        lw = (norm1_g[l], w_in[l], w_dw[l], b_dw[l], conv_ln_g[l], conv_ln_b[l],
              w_conv_out[l], b_conv_out[l], lambda_q1[l], lambda_k1[l], lambda_q2[l],
              lambda_k2[l], subln_g[l], w_attn_out[l], w_o[l], norm2_g[l], w_router[l],
              b_router[l], w_exp_gate[l], b_exp_gate[l], w_exp_up[l], b_exp_up[l],
              w_exp_down[l], b_exp_down[l])
        conv_zero = jnp.zeros((yp.shape[0], CONV_WIDTH - 1, D_CONV), yp.dtype)
        yp, kp, vp, cp = decoder_layer(yp, conv_zero, None, None, lam_init, *lw)
        k_past = cache_k[l, page_table].reshape(dec_b, -1, 2 * N_HEADS, HEAD_DIM)
        v_past = cache_v[l, page_table].reshape(dec_b, -1, N_HEADS, V_DIM)
        ys, kn, vn, cn = decoder_layer(ys, state_conv[l], k_past, v_past, lam_init, *lw)
        kp_l.append(kp); vp_l.append(vp); cp_l.append(cp)
        ks_l.append(kn); vs_l.append(vn); cs_l.append(cn)
    y_prompt = rmsnorm(yp, final_norm_g)
    y_sample = rmsnorm(ys, final_norm_g)
    return (y_prompt, y_sample, jnp.stack(kp_l), jnp.stack(vp_l), jnp.stack(cp_l),
            jnp.stack(ks_l), jnp.stack(vs_l), jnp.stack(cs_l))
```

```python
import functools
import math

import jax
import jax.numpy as jnp
from jax import lax
from jax.experimental import pallas as pl
from jax.experimental.pallas import tpu as pltpu

RMS_EPS = 1e-5
LN_EPS = 1e-5
CONV_WIDTH = 31
CONV_HALO = 32
N_HEADS = 8
HEAD_DIM = 64
V_DIM = 128
HEAD_SHIFT = 6
TOP_K = 4
SWIGLU_ALPHA = 1.702
SWIGLU_LIMIT = 7.0
PAGE_SIZE = 128
NEG_BIG = -1e30

V7X_VMEM_LIMIT_BYTES = 56 * 1024 * 1024
BF16 = jnp.bfloat16
F32 = jnp.float32


def _params(n_axes):
    return pltpu.CompilerParams(dimension_semantics=("arbitrary",) * n_axes,
                                vmem_limit_bytes=V7X_VMEM_LIMIT_BYTES)


def _rms(x, g):
    return x * lax.rsqrt(jnp.mean(x * x, axis=-1, keepdims=True) + RMS_EPS) * g


def _lam_value(lq1, lk1, lq2, lk2, lam_init):
    s1 = jnp.sum(lq1 * lk1, axis=-1, keepdims=True)
    s2 = jnp.sum(lq2 * lk2, axis=-1, keepdims=True)
    return jnp.exp(s1) - jnp.exp(s2) + lam_init


def _inproj_kernel(x_ref, g_ref, w_ref, z_ref, h_sc):
    @pl.when(pl.program_id(1) == 0)
    def _():
        h_sc[...] = _rms(x_ref[...], g_ref[...]).astype(BF16)

    z_ref[...] = jnp.dot(h_sc[...], w_ref[...], preferred_element_type=F32)


def _inproj(x, g, w_bf16, tm, tn):
    t, d = x.shape
    n = w_bf16.shape[1]
    return pl.pallas_call(
        _inproj_kernel,
        grid=(t // tm, n // tn),
        in_specs=[pl.BlockSpec((tm, d), lambda i, j: (i, 0)),
                  pl.BlockSpec((1, d), lambda i, j: (0, 0)),
                  pl.BlockSpec((d, tn), lambda i, j: (0, j))],
        out_specs=pl.BlockSpec((tm, tn), lambda i, j: (i, j)),
        out_shape=jax.ShapeDtypeStruct((t, n), F32),
        scratch_shapes=[pltpu.VMEM((tm, d), BF16)],
        compiler_params=_params(2),
        name="inproj",
    )(x, g.reshape(1, d), w_bf16)


def _ln_silu(y, lng, lnb):
    mu = jnp.mean(y, axis=-1, keepdims=True)
    yc = y - mu
    var = jnp.mean(yc * yc, axis=-1, keepdims=True)
    yn = yc * lax.rsqrt(var + LN_EPS) * lng + lnb
    return yn * jax.nn.sigmoid(yn)


def _conv_prompt_kernel(a_ref, g_ref, wdw_ref, bdw_ref, lng_ref, lnb_ref, u_ref, c_ref, win_sc, *, rows):
    ts = a_ref.shape[0]

    @pl.when(pl.program_id(1) == 0)
    def _():
        win_sc[0:CONV_HALO, :] = jnp.zeros((CONV_HALO, win_sc.shape[1]), F32)

    u = a_ref[...] * jax.nn.sigmoid(g_ref[...])
    u_ref[...] = u
    win_sc[CONV_HALO:CONV_HALO + ts, :] = u
    off = CONV_HALO - (CONV_WIDTH - 1)
    for r0 in range(0, ts, rows):
        acc = jnp.zeros((rows, u.shape[1]), F32)
        for j in range(CONV_WIDTH):
            acc = acc + wdw_ref[j:j + 1, :] * win_sc[r0 + off + j:r0 + off + j + rows, :]
        y = acc + bdw_ref[...]
        c_ref[r0:r0 + rows, :] = _ln_silu(y, lng_ref[...], lnb_ref[...]).astype(BF16)
    win_sc[0:CONV_HALO, :] = win_sc[ts:ts + CONV_HALO, :]


def _conv_prompt(z, batch, seq, dc, w_dw, b_dw, ln_g, ln_b, ts):
    t = z.shape[0]
    nt = seq // ts
    vec = lambda v: v.reshape(1, dc)
    const = lambda shape: pl.BlockSpec(shape, lambda b, i: (0, 0))
    return pl.pallas_call(
        functools.partial(_conv_prompt_kernel, rows=32),
        grid=(batch, nt),
        in_specs=[pl.BlockSpec((ts, dc), lambda b, i: (b * nt + i, 0)),
                  pl.BlockSpec((ts, dc), lambda b, i: (b * nt + i, 1)),
                  const((CONV_WIDTH, dc)), const((1, dc)), const((1, dc)), const((1, dc))],
        out_specs=[pl.BlockSpec((ts, dc), lambda b, i: (b * nt + i, 0)),
                   pl.BlockSpec((ts, dc), lambda b, i: (b * nt + i, 0))],
        out_shape=[jax.ShapeDtypeStruct((t, dc), F32), jax.ShapeDtypeStruct((t, dc), BF16)],
        scratch_shapes=[pltpu.VMEM((ts + CONV_HALO, dc), F32)],
        compiler_params=_params(2),
        name="conv_prompt",
    )(z, z, w_dw, vec(b_dw), vec(ln_g), vec(ln_b))


def _conv_sample_kernel(a_ref, g_ref, st_ref, wdw_ref, bdw_ref, lng_ref, lnb_ref, u_ref, c_ref):
    nb = a_ref.shape[0]
    hist = CONV_WIDTH - 1
    u = a_ref[...] * jax.nn.sigmoid(g_ref[...])
    u_ref[...] = u

    def body(b, carry):
        past = jnp.sum(st_ref[b] * wdw_ref[0:hist, :], axis=0, keepdims=True)
        ub = u_ref[pl.ds(b, 1), :]
        y = past + ub * wdw_ref[hist:hist + 1, :] + bdw_ref[...]
        c_ref[pl.ds(b, 1), :] = _ln_silu(y, lng_ref[...], lnb_ref[...])
        return carry

    lax.fori_loop(0, nb, body, 0)


def _conv_sample(z, state, dc, w_dw, b_dw, ln_g, ln_b):
    nb = z.shape[0]
    vec = lambda v: v.reshape(1, dc)
    const = lambda shape: pl.BlockSpec(shape, lambda i: (0,) * len(shape))
    return pl.pallas_call(
        _conv_sample_kernel,
        grid=(1,),
        in_specs=[pl.BlockSpec((nb, dc), lambda i: (0, 0)),
                  pl.BlockSpec((nb, dc), lambda i: (0, 1)),
                  const(state.shape), const((CONV_WIDTH, dc)), const((1, dc)), const((1, dc)), const((1, dc))],
        out_specs=[const((nb, dc)), const((nb, dc))],
        out_shape=[jax.ShapeDtypeStruct((nb, dc), F32), jax.ShapeDtypeStruct((nb, dc), F32)],
        compiler_params=_params(1),
        name="conv_sample",
    )(z, z, state, w_dw, vec(b_dw), vec(ln_g), vec(ln_b))


def _attn_prompt_kernel(slopes_ref, q_ref, k_ref, v_ref, lq1_ref, lk1_ref, lq2_ref, lk2_ref, g_ref, o_ref,
                        *, lam_init):
    tq = q_ref.shape[0]
    h = pl.program_id(1)
    qi = pl.program_id(2)
    slope = slopes_ref[h]
    lam = _lam_value(lq1_ref[...], lk1_ref[...], lq2_ref[...], lk2_ref[...], lam_init)

    q = (q_ref[...] * (HEAD_DIM ** -0.5)).astype(BF16)
    lane = lax.broadcasted_iota(jnp.int32, q.shape, 1)
    zero = jnp.zeros_like(q)
    qq = jnp.concatenate([jnp.where(lane < HEAD_DIM, q, zero), jnp.where(lane >= HEAD_DIM, q, zero)], axis=0)

    q0 = qi * tq
    col = lax.broadcasted_iota(jnp.int32, (1, tq), 1)

    def step(j, carry, diagonal):
        m, l, acc = carry
        kb = k_ref[pl.ds(pl.multiple_of(j * tq, tq), tq), :].astype(BF16)
        vb = v_ref[pl.ds(pl.multiple_of(j * tq, tq), tq), :].astype(BF16)
        s = lax.dot_general(qq, kb, (((1,), (1,)), ((), ())), preferred_element_type=F32)
        s = s + slope * (j * tq + col - q0).astype(F32)
        if diagonal:
            r = lax.broadcasted_iota(jnp.int32, s.shape, 0)
            r = jnp.where(r >= tq, r - tq, r)
            c = lax.broadcasted_iota(jnp.int32, s.shape, 1)
            s = jnp.where(c <= r, s, -jnp.inf)
        m_new = jnp.maximum(m, jnp.max(s, axis=1, keepdims=True))
        alpha = jnp.exp(m - m_new)
        p = jnp.exp(s - m_new)
        l = alpha * l + jnp.sum(p, axis=1, keepdims=True)
        acc = alpha * acc + jnp.dot(p.astype(BF16), vb, preferred_element_type=F32)
        return m_new, l, acc

    init = (jnp.full((2 * tq, 1), NEG_BIG, F32), jnp.zeros((2 * tq, 1), F32), jnp.zeros((2 * tq, V_DIM), F32))
    carry = lax.fori_loop(0, qi, lambda j, c: step(j, c, False), init)
    _, l, acc = step(qi, carry, True)
    o = acc / l
    d = o[:tq] - lam * o[tq:]
    o_ref[...] = (_rms(d, g_ref[...]) * (1.0 - lam_init)).astype(BF16)


def _attn_prompt(z, batch, seq, col_q, col_k, col_v, lams, subln_g, lam_init, tq):
    t = z.shape[0]
    nq = seq // tq
    slopes = 2.0 ** (-8.0 * jnp.arange(1, N_HEADS + 1, dtype=F32) / N_HEADS)
    blk = 2 * HEAD_DIM
    lam_spec = pl.BlockSpec((1, HEAD_DIM), lambda b, h, i, s: (0, 0))
    grid_spec = pltpu.PrefetchScalarGridSpec(
        num_scalar_prefetch=1,
        grid=(batch, N_HEADS, nq),
        in_specs=[pl.BlockSpec((tq, blk), lambda b, h, i, s: (b * nq + i, col_q // blk + h)),
                  pl.BlockSpec((seq, blk), lambda b, h, i, s: (b, col_k // blk + h)),
                  pl.BlockSpec((seq, V_DIM), lambda b, h, i, s: (b, col_v // V_DIM + h)),
                  lam_spec, lam_spec, lam_spec, lam_spec,
                  pl.BlockSpec((1, V_DIM), lambda b, h, i, s: (0, 0))],
        out_specs=pl.BlockSpec((tq, V_DIM), lambda b, h, i, s: (b * nq + i, h)),
    )
    return pl.pallas_call(
        functools.partial(_attn_prompt_kernel, lam_init=lam_init),
        grid_spec=grid_spec,
        out_shape=jax.ShapeDtypeStruct((t, N_HEADS * V_DIM), BF16),
        compiler_params=_params(3),
        name="attn_prompt",
    )(slopes, z, z, z, *[v.reshape(1, HEAD_DIM) for v in lams], subln_g.reshape(1, V_DIM))


def _attn_decode_kernel(pt_ref, q_ref, kn_ref, vn_ref, spread_ref, kc_ref, vc_ref, o_ref, m_sc, l_sc, acc_sc,
                        *, past_len):
    p = pl.program_id(1)
    n_pages = pl.num_programs(1)
    n_maps = 2 * N_HEADS
    keys = kc_ref.shape[2]

    @pl.when(p == 0)
    def _():
        m_sc[...] = jnp.full(m_sc.shape, NEG_BIG, F32)
        l_sc[...] = jnp.zeros(l_sc.shape, F32)
        acc_sc[...] = jnp.zeros(acc_sc.shape, F32)

    q_row = q_ref[...] * (HEAD_DIM ** -0.5)
    qrow_i = lax.broadcasted_iota(jnp.int32, (n_maps, n_maps * HEAD_DIM), 0)
    qcol_i = lax.broadcasted_iota(jnp.int32, (n_maps, n_maps * HEAD_DIM), 1)
    qbd = jnp.where((qcol_i >> HEAD_SHIFT) == qrow_i, q_row, 0.0).astype(BF16)
    row1 = lax.broadcasted_iota(jnp.int32, (n_maps, 1), 0)
    slope = jnp.exp2(-((row1 >> 1) + 1).astype(F32))

    kb = kc_ref[...].reshape(n_maps * HEAD_DIM, keys).astype(BF16)
    s = jnp.dot(qbd, kb, preferred_element_type=F32)
    k_pos = p * keys + lax.broadcasted_iota(jnp.int32, s.shape, 1)
    s = s - slope * (past_len - k_pos).astype(F32)

    m_prev = m_sc[...]
    m_new = jnp.maximum(m_prev, jnp.max(s, axis=1, keepdims=True))
    alpha = jnp.exp(m_prev - m_new)
    pe = jnp.exp(s - m_new)
    l_sc[...] = alpha * l_sc[...] + jnp.sum(pe, axis=1, keepdims=True)
    m_sc[...] = m_new

    spread = jnp.dot(pe.astype(BF16), spread_ref[...], preferred_element_type=F32)
    srow = lax.broadcasted_iota(jnp.int32, spread.shape, 0)
    scol = lax.broadcasted_iota(jnp.int32, spread.shape, 1)
    pc = jnp.where((scol & (N_HEADS - 1)) == (srow >> 1), spread, 0.0).astype(BF16)
    vb = vc_ref[...].reshape(keys * N_HEADS, V_DIM).astype(BF16)
    acc_sc[...] = alpha * acc_sc[...] + jnp.dot(pc, vb, preferred_element_type=F32)

    @pl.when(p == n_pages - 1)
    def _():
        kn = kn_ref[...].astype(BF16).astype(F32)
        s_new = jnp.sum(qbd.astype(F32) * kn, axis=1, keepdims=True)
        m_old = m_sc[...]
        m_fin = jnp.maximum(m_old, s_new)
        a = jnp.exp(m_old - m_fin)
        p_new = jnp.exp(s_new - m_fin)
        l_fin = a * l_sc[...] + p_new
        vn = vn_ref[...].astype(BF16).astype(F32)
        acc = a * acc_sc[...] + p_new.astype(BF16).astype(F32) * vn
        o_ref[...] = acc / l_fin


def _attn_decode(q_s, kn_s, vn_dup, cache_kt, cache_v, layer, page_table, past_len):
    nb, n_pages = page_table.shape
    n_maps = 2 * N_HEADS
    page = cache_kt.shape[-1]
    spread = (jnp.arange(page)[:, None] == jnp.arange(page * N_HEADS)[None, :] // N_HEADS).astype(BF16)
    grid_spec = pltpu.PrefetchScalarGridSpec(
        num_scalar_prefetch=1,
        grid=(nb, n_pages),
        in_specs=[pl.BlockSpec((None, 1, n_maps * HEAD_DIM), lambda b, p, pt: (b, 0, 0)),
                  pl.BlockSpec((None, 1, n_maps * HEAD_DIM), lambda b, p, pt: (b, 0, 0)),
                  pl.BlockSpec((None, n_maps, V_DIM), lambda b, p, pt: (b, 0, 0)),
                  pl.BlockSpec((page, page * N_HEADS), lambda b, p, pt: (0, 0)),
                  pl.BlockSpec((None, None, n_maps, HEAD_DIM, page),
                               lambda b, p, pt: (layer, pt[b * n_pages + p], 0, 0, 0)),
                  pl.BlockSpec((None, None, page, N_HEADS, V_DIM),
                               lambda b, p, pt: (layer, pt[b * n_pages + p], 0, 0, 0))],
        out_specs=pl.BlockSpec((None, n_maps, V_DIM), lambda b, p, pt: (b, 0, 0)),
        scratch_shapes=[pltpu.VMEM((n_maps, 1), F32), pltpu.VMEM((n_maps, 1), F32),
                        pltpu.VMEM((n_maps, V_DIM), F32)],
    )
    return pl.pallas_call(
        functools.partial(_attn_decode_kernel, past_len=past_len),
        grid_spec=grid_spec,
        out_shape=jax.ShapeDtypeStruct((nb, n_maps, V_DIM), F32),
        compiler_params=_params(2),
        name="attn_decode",
    )(page_table.reshape(-1), q_s, kn_s, vn_dup, spread, cache_kt, cache_v)


def _diff_norm_kernel(o_ref, lq1_ref, lk1_ref, lq2_ref, lk2_ref, g_ref, out_ref, *, lam_init):
    lam = _lam_value(lq1_ref[...], lk1_ref[...], lq2_ref[...], lk2_ref[...], lam_init)
    o = o_ref[...]
    d = o[:, :V_DIM] - lam * o[:, V_DIM:]
    out_ref[...] = _rms(d, g_ref[...]) * (1.0 - lam_init)


def _diff_norm(o_pairs, lams, subln_g, lam_init):
    rows = o_pairs.shape[0]
    const = lambda shape: pl.BlockSpec(shape, lambda i: (0, 0))
    return pl.pallas_call(
        functools.partial(_diff_norm_kernel, lam_init=lam_init),
        grid=(1,),
        in_specs=[const(o_pairs.shape)] + [const((1, HEAD_DIM))] * 4 + [const((1, V_DIM))],
        out_specs=const((rows, V_DIM)),
        out_shape=jax.ShapeDtypeStruct((rows, V_DIM), F32),
        compiler_params=_params(1),
        name="diff_norm",
    )(o_pairs, *[v.reshape(1, HEAD_DIM) for v in lams], subln_g.reshape(1, V_DIM))


def _merge_kernel(x_ref, c_ref, o_ref, gc0_ref, gc1_ref, ga0_ref, ga1_ref, wco_ref, bco_ref, wao_ref, wo_ref,
                  g2_ref, x1_ref, h2_ref):
    half = gc0_ref.shape[1]
    conv_out = jnp.dot(c_ref[...].astype(BF16), wco_ref[...], preferred_element_type=F32) + bco_ref[...]
    attn_out = jnp.dot(o_ref[...].astype(BF16), wao_ref[...], preferred_element_type=F32)
    mixed = []
    for n, (gc_ref, ga_ref) in enumerate(((gc0_ref, ga0_ref), (gc1_ref, ga1_ref))):
        sl = slice(n * half, (n + 1) * half)
        mixed.append((jax.nn.sigmoid(gc_ref[...]) * conv_out[:, sl]
                      + jax.nn.sigmoid(ga_ref[...]) * attn_out[:, sl]).astype(BF16))
    x1 = x_ref[...] + jnp.dot(jnp.concatenate(mixed, axis=1), wo_ref[...], preferred_element_type=F32)
    x1_ref[...] = x1
    h2_ref[...] = _rms(x1, g2_ref[...])


def _merge(x, c_act, o_norm, z, col_gc, col_ga, wco, bco, wao, wo, g2, tm):
    t, d = x.shape
    dc, da = c_act.shape[1], o_norm.shape[1]
    half = d // 2
    row = lambda w: pl.BlockSpec((tm, w), lambda i: (i, 0))
    gate = lambda blk: pl.BlockSpec((tm, half), lambda i: (i, blk))
    weight = lambda shape: pl.BlockSpec(shape, lambda i: (0, 0), pipeline_mode=pl.Buffered(1))
    return pl.pallas_call(
        _merge_kernel,
        grid=(t // tm,),
        in_specs=[row(d), row(dc), row(da),
                  gate(col_gc // half), gate(col_gc // half + 1), gate(col_ga // half), gate(col_ga // half + 1),
                  weight((dc, d)), weight((1, d)), weight((da, d)), weight((d, d)), weight((1, d))],
        out_specs=[row(d), row(d)],
        out_shape=[jax.ShapeDtypeStruct((t, d), F32), jax.ShapeDtypeStruct((t, d), F32)],
        compiler_params=_params(1),
        name="merge",
    )(x, c_act, o_norm, z, z, z, z, wco, bco.reshape(1, d), wao, wo, g2.reshape(1, d))


def _router_kernel(h_ref, wr_ref, br_ref, base_ref, tri_ref, ti_ref, tw_ref, rk_ref, cnt_ref, cnt_sc, *, n_tokens):
    i = pl.program_id(0)
    tb = h_ref.shape[0]
    n_exp = wr_ref.shape[0]

    @pl.when(i == 0)
    def _():
        cnt_sc[...] = base_ref[...]

    logits = lax.dot_general(wr_ref[...], h_ref[...].astype(BF16), (((1,), (1,)), ((), ())),
                             preferred_element_type=F32) + br_ref[...]
    tok = i * tb + lax.broadcasted_iota(jnp.int32, (1, tb), 1)
    valid = tok < n_tokens
    eidx = lax.broadcasted_iota(jnp.int32, (n_exp, tb), 0)
    run = cnt_sc[...]
    x = logits
    vals = []
    for k in range(TOP_K):
        mx = jnp.max(x, axis=0, keepdims=True)
        sel = jnp.min(jnp.where(x == mx, eidx, n_exp), axis=0, keepdims=True)
        hit = eidx == sel
        x = jnp.where(hit, -jnp.inf, x)
        onehot = jnp.where(hit & valid, 1.0, 0.0)
        before = jnp.dot(onehot.astype(BF16), tri_ref[...], preferred_element_type=F32)
        rank = jnp.sum(onehot * (before + run), axis=0, keepdims=True)
        run = run + jnp.sum(onehot, axis=1, keepdims=True)
        vals.append(mx)
        ti_ref[k:k + 1, :] = sel
        rk_ref[k:k + 1, :] = rank.astype(jnp.int32)
    es = [jnp.exp(v - vals[0]) for v in vals]
    tot = es[0] + es[1] + es[2] + es[3]
    for k in range(TOP_K):
        tw_ref[k:k + 1, :] = es[k] / tot
    cnt_sc[...] = run
    cnt_ref[...] = run


def _router(h2, wr_t, b_router, base, tb):
    t, d = h2.shape
    n_exp = wr_t.shape[0]
    nb = pl.cdiv(t, tb)
    tri = (jnp.arange(tb)[:, None] < jnp.arange(tb)[None, :]).astype(BF16)
    const = lambda shape: pl.BlockSpec(shape, lambda i: (0, 0))
    tok = lambda dt: jax.ShapeDtypeStruct((TOP_K, t), dt)
    return pl.pallas_call(
        functools.partial(_router_kernel, n_tokens=t),
        grid=(nb,),
        in_specs=[pl.BlockSpec((tb, d), lambda i: (i, 0)), const((n_exp, d)), const((n_exp, 1)),
                  const((n_exp, 1)), const((tb, tb))],
        out_specs=[pl.BlockSpec((TOP_K, tb), lambda i: (0, i))] * 3 + [const((n_exp, 1))],
        out_shape=[tok(jnp.int32), tok(F32), tok(jnp.int32), jax.ShapeDtypeStruct((n_exp, 1), F32)],
        scratch_shapes=[pltpu.VMEM((n_exp, 1), F32)],
        compiler_params=_params(1),
        name="router",
    )(h2, wr_t, b_router.reshape(n_exp, 1), base, tri)


def _block_index_list(dest, tb):
    t = dest.shape[1]
    return dest.reshape(TOP_K, t // tb, tb).transpose(1, 0, 2).reshape(-1)


def _dispatch_kernel(dest_ref, h_ref, xs_in_ref, xs_ref, idx_sm, idx_sem, row_sem):
    del xs_in_ref
    i = pl.program_id(0)
    tb = h_ref.shape[0]
    n_idx = TOP_K * tb
    cp = pltpu.make_async_copy(dest_ref.at[pl.ds(pl.multiple_of(i * n_idx, n_idx), n_idx)], idx_sm, idx_sem)
    cp.start()
    cp.wait()

    def row_copy(t, k):
        d = idx_sm[k * tb + t]
        return pltpu.make_async_copy(h_ref.at[pl.ds(t, 1), :], xs_ref.at[pl.ds(d, 1), :], row_sem)

    def issue(t, carry):
        for k in range(TOP_K):
            row_copy(t, k).start()
        return carry

    def drain(t, carry):
        for k in range(TOP_K):
            row_copy(t, k).wait()
        return carry

    lax.fori_loop(0, tb, issue, 0)
    lax.fori_loop(0, tb, drain, 0)


def _dispatch(h2, dest, xs, tb):
    t, d = h2.shape
    return pl.pallas_call(
        _dispatch_kernel,
        grid=(t // tb,),
        in_specs=[pl.BlockSpec(memory_space=pl.ANY),
                  pl.BlockSpec((tb, d), lambda i: (i, 0)),
                  pl.BlockSpec(memory_space=pl.ANY)],
        out_specs=pl.BlockSpec(memory_space=pl.ANY),
        out_shape=jax.ShapeDtypeStruct(xs.shape, xs.dtype),
        scratch_shapes=[pltpu.SMEM((TOP_K * tb,), jnp.int32), pltpu.SemaphoreType.DMA(()),
                        pltpu.SemaphoreType.DMA(())],
        input_output_aliases={2: 0},
        compiler_params=_params(1),
        name="moe_dispatch",
    )(_block_index_list(dest, tb), h2, xs)


def _combine_kernel(dest_ref, x1_ref, w_ref, g_ref, ys_ref, y_ref, idx_sm, rows_sc, idx_sem, row_sem, *, normalize):
    i = pl.program_id(0)
    tb = x1_ref.shape[0]
    n_idx = TOP_K * tb
    cp = pltpu.make_async_copy(dest_ref.at[pl.ds(pl.multiple_of(i * n_idx, n_idx), n_idx)], idx_sm, idx_sem)
    cp.start()
    cp.wait()

    def row_copy(t, k):
        d = idx_sm[k * tb + t]
        return pltpu.make_async_copy(ys_ref.at[pl.ds(d, 1), :], rows_sc.at[k, pl.ds(t, 1), :], row_sem)

    def issue(t, carry):
        for k in range(TOP_K):
            row_copy(t, k).start()
        return carry

    def drain(t, carry):
        for k in range(TOP_K):
            row_copy(t, k).wait()
        return carry

    lax.fori_loop(0, tb, issue, 0)
    lax.fori_loop(0, tb, drain, 0)
    y = x1_ref[...]
    for k in range(TOP_K):
        y = y + w_ref[:, k:k + 1] * rows_sc[k]
    y_ref[...] = _rms(y, g_ref[...]) if normalize else y


def _combine(x1, dest, top_w, ys, final_g, normalize, tb):
    t, d = x1.shape
    return pl.pallas_call(
        functools.partial(_combine_kernel, normalize=normalize),
        grid=(t // tb,),
        in_specs=[pl.BlockSpec(memory_space=pl.ANY),
                  pl.BlockSpec((tb, d), lambda i: (i, 0)),
                  pl.BlockSpec((tb, TOP_K), lambda i: (i, 0)),
                  pl.BlockSpec((1, d), lambda i: (0, 0)),
                  pl.BlockSpec(memory_space=pl.ANY)],
        out_specs=pl.BlockSpec((tb, d), lambda i: (i, 0)),
        out_shape=jax.ShapeDtypeStruct((t, d), F32),
        scratch_shapes=[pltpu.SMEM((TOP_K * tb,), jnp.int32), pltpu.VMEM((TOP_K, tb, d), F32),
                        pltpu.SemaphoreType.DMA(()), pltpu.SemaphoreType.DMA(())],
        compiler_params=_params(1),
        name="moe_combine",
    )(_block_index_list(dest, tb), x1, top_w.T, final_g.reshape(1, d), ys)


def _group_is_new(blk_e_ref, i, ic):
    return (i == 0) | (blk_e_ref[ic] != blk_e_ref[jnp.maximum(ic - 1, 0)])


def _moe_gu_kernel(blk_e_ref, nact_ref, x_ref, wg_ref, wu_ref, bg_ref, bu_ref, h_ref, wg_sc, wu_sc):
    i = pl.program_id(1)
    nact = nact_ref[0]
    ic = jnp.minimum(i, nact - 1)
    active = i < nact

    @pl.when(active & _group_is_new(blk_e_ref, i, ic))
    def _():
        wg_sc[...] = wg_ref[...].astype(BF16)
        wu_sc[...] = wu_ref[...].astype(BF16)

    @pl.when(active)
    def _():
        x = x_ref[...].astype(BF16)
        gt = jnp.minimum(jnp.dot(x, wg_sc[...], preferred_element_type=F32) + bg_ref[...], SWIGLU_LIMIT)
        up = jnp.clip(jnp.dot(x, wu_sc[...], preferred_element_type=F32) + bu_ref[...], -SWIGLU_LIMIT, SWIGLU_LIMIT)
        h_ref[...] = (gt * jax.nn.sigmoid(SWIGLU_ALPHA * gt) * (up + 1.0)).astype(BF16)

    @pl.when(jnp.logical_not(active))
    def _():
        h_ref[...] = jnp.zeros(h_ref.shape, h_ref.dtype)


def _moe_down_kernel(blk_e_ref, nact_ref, h_ref, wd_ref, bd_ref, y_ref, wd_sc):
    i = pl.program_id(1)
    nact = nact_ref[0]
    ic = jnp.minimum(i, nact - 1)
    active = i < nact

    @pl.when(active & _group_is_new(blk_e_ref, i, ic))
    def _():
        wd_sc[...] = wd_ref[...].astype(BF16)

    @pl.when(active)
    def _():
        y_ref[...] = jnp.dot(h_ref[...], wd_sc[...], preferred_element_type=F32) + bd_ref[...]

    @pl.when(jnp.logical_not(active))
    def _():
        y_ref[...] = jnp.zeros(y_ref.shape, y_ref.dtype)


def _moe_experts(xs, blk_e, nact, layer, wg, bg, wu, bu, wd, bd, tmb, tn):
    rows, d = xs.shape
    _, n_exp, _, de = wg.shape
    nblk = rows // tmb
    lhs = lambda w: pl.BlockSpec((tmb, w), lambda n, i, be, na: (jnp.minimum(i, na[0] - 1), 0))
    wspec = lambda k: pl.BlockSpec((None, None, k, tn),
                                   lambda n, i, be, na: (layer, be[jnp.minimum(i, na[0] - 1)], 0, n))
    bspec = pl.BlockSpec((None, 1, tn), lambda n, i, be, na: (be[jnp.minimum(i, na[0] - 1)], 0, n))
    out = pl.BlockSpec((tmb, tn), lambda n, i, be, na: (i, n))
    bg, bu, bd = bg[layer], bu[layer], bd[layer]
    hidden = pl.pallas_call(
        _moe_gu_kernel,
        grid_spec=pltpu.PrefetchScalarGridSpec(
            num_scalar_prefetch=2, grid=(de // tn, nblk),
            in_specs=[lhs(d), wspec(d), wspec(d), bspec, bspec], out_specs=out,
            scratch_shapes=[pltpu.VMEM((d, tn), BF16), pltpu.VMEM((d, tn), BF16)]),
        out_shape=jax.ShapeDtypeStruct((rows, de), BF16),
        compiler_params=_params(2),
        name="moe_gate_up",
    )(blk_e, nact, xs, wg, wu, bg.reshape(n_exp, 1, de), bu.reshape(n_exp, 1, de))
    return pl.pallas_call(
        _moe_down_kernel,
        grid_spec=pltpu.PrefetchScalarGridSpec(
            num_scalar_prefetch=2, grid=(d // tn, nblk),
            in_specs=[lhs(de), wspec(de), bspec], out_specs=out,
            scratch_shapes=[pltpu.VMEM((de, tn), BF16)]),
        out_shape=jax.ShapeDtypeStruct((rows, d), F32),
        compiler_params=_params(2),
        name="moe_down",
    )(blk_e, nact, hidden, wd, bd.reshape(n_exp, 1, d))


def _tile(n, pref):
    return pref if n % pref == 0 else n


def _token_mixers(x, is_prompt, batch, seq, lw, lam_init, sample_state):
    (norm1_g, w_in, w_dw, b_dw, conv_ln_g, conv_ln_b, w_conv_out, b_conv_out, lams, subln_g, w_attn_out, w_o,
     norm2_g) = lw
    t, d = x.shape
    dc = w_dw.shape[1]
    da = N_HEADS * V_DIM
    col_q, col_k, col_v = 2 * dc, 2 * dc + da, 2 * dc + 2 * da
    col_gc, col_ga = 2 * dc + 3 * da, 2 * dc + 3 * da + d
    z = _inproj(x, norm1_g, w_in, _tile(t, 512), 1536)
    k_new = z[:, col_k:col_v].reshape(batch, seq, 2 * N_HEADS, HEAD_DIM)
    v_new = z[:, col_v:col_gc].reshape(batch, seq, N_HEADS, V_DIM)
    if is_prompt:
        u, c_act = _conv_prompt(z, batch, seq, dc, w_dw, b_dw, conv_ln_g, conv_ln_b, _tile(seq, 256))
        conv_new = u.reshape(batch, seq, dc)[:, seq - (CONV_WIDTH - 1):]
        o_norm = _attn_prompt(z, batch, seq, col_q, col_k, col_v, lams, subln_g, lam_init, _tile(seq, 256))
    else:
        state_conv, cache_k, cache_v, layer, page_table, past_len = sample_state
        u, c_act = _conv_sample(z, state_conv, dc, w_dw, b_dw, conv_ln_g, conv_ln_b)
        conv_new = jnp.concatenate([state_conv[:, 1:], u[:, None, :]], axis=1)
        vn_dup = jnp.repeat(v_new.reshape(t, N_HEADS, V_DIM), 2, axis=1)
        cache_kt = jnp.transpose(cache_k, (0, 1, 3, 4, 2))
        o_maps = _attn_decode(z[:, col_q:col_k].reshape(t, 1, da), z[:, col_k:col_v].reshape(t, 1, da), vn_dup,
                              cache_kt, cache_v, layer, page_table, past_len)
        o_norm = _diff_norm(o_maps.reshape(t * N_HEADS, 2 * V_DIM), lams, subln_g, lam_init).reshape(t, da)
    x1, h2 = _merge(x, c_act, o_norm, z, col_gc, col_ga, w_conv_out, b_conv_out, w_attn_out, w_o, norm2_g,
                    _tile(t, 256))
    return x1, h2, k_new, v_new, conv_new


def _moe(groups, w_router, b_router, layer, wg, bg, wu, bu, wd, bd, final_g, normalize):
    _, n_exp, d, _ = wg.shape
    tmb = 256
    wr_t = w_router.T.astype(BF16)
    base = jnp.zeros((n_exp, 1), F32)
    routed = []
    for x1, h2 in groups:
        top_i, top_w, rank, base = _router(h2, wr_t, b_router, base, _tile(h2.shape[0], 512))
        routed.append((top_i, top_w, rank))
    counts = base[:, 0].astype(jnp.int32)
    padded = (counts + tmb - 1) // tmb * tmb
    pend = jnp.cumsum(padded)
    pstart = pend - padded
    n_assign = sum(h2.shape[0] for _, h2 in groups) * TOP_K
    nblk = pl.cdiv(n_assign, tmb) + n_exp
    nact = (pend[-1] // tmb).astype(jnp.int32).reshape(1)
    blk_row = jnp.arange(nblk, dtype=jnp.int32) * tmb
    blk_e = jnp.minimum(jnp.sum(pend[None, :] <= blk_row[:, None], axis=1), n_exp - 1).astype(jnp.int32)
    xs = jnp.zeros((nblk * tmb, d), F32)
    dests = []
    for (x1, h2), (top_i, top_w, rank) in zip(groups, routed):
        dest = pstart[top_i] + rank
        dests.append(dest)
        xs = _dispatch(h2, dest, xs, _tile(h2.shape[0], 256))
    ys = _moe_experts(xs, blk_e, nact, layer, wg, bg, wu, bu, wd, bd, tmb, 512)
    outs = []
    for (x1, h2), (top_i, top_w, rank), dest in zip(groups, routed, dests):
        outs.append(_combine(x1, dest, top_w, ys, final_g, normalize, _tile(x1.shape[0], 256)))
    return outs


def kernel(x_prompt, x_sample, cache_k, cache_v, state_conv, page_table, norm1_g, w_in, w_dw, b_dw, conv_ln_g, conv_ln_b, w_conv_out, b_conv_out, lambda_q1, lambda_k1, lambda_q2, lambda_k2, subln_g, w_attn_out, w_o, norm2_g, w_router, b_router, w_exp_gate, b_exp_gate, w_exp_up, b_exp_up, w_exp_down, b_exp_down, final_norm_g):
    depth = w_in.shape[0]
    batch, seq, d = x_prompt.shape
    dec_b, dec_seq, _ = x_sample.shape
    assert dec_seq == 1
    past_len = page_table.shape[1] * cache_k.shape[2]
    xp = x_prompt.reshape(batch * seq, d)
    xs = x_sample.reshape(dec_b * dec_seq, d)
    outs = [[] for _ in range(6)]
    for l in range(depth):
        lam_init = 0.8 - 0.6 * math.exp(-0.3 * l)
        lams = (lambda_q1[l], lambda_k1[l], lambda_q2[l], lambda_k2[l])
        lw = (norm1_g[l], w_in[l].astype(BF16), w_dw[l], b_dw[l], conv_ln_g[l], conv_ln_b[l],
              w_conv_out[l].astype(BF16), b_conv_out[l], lams, subln_g[l], w_attn_out[l].astype(BF16),
              w_o[l].astype(BF16), norm2_g[l])
        x1p, h2p, kp, vp, cp = _token_mixers(xp, True, batch, seq, lw, lam_init, None)
        x1s, h2s, ks, vs, cs = _token_mixers(xs, False, dec_b, dec_seq, lw, lam_init,
                                             (state_conv[l], cache_k, cache_v, l, page_table, past_len))
        xp, xs = _moe([(x1p, h2p), (x1s, h2s)], w_router[l], b_router[l], l, w_exp_gate, b_exp_gate,
                      w_exp_up, b_exp_up, w_exp_down, b_exp_down, final_norm_g, l == depth - 1)
        for lst, v in zip(outs, (kp, vp, cp, ks, vs, cs)):
            lst.append(v)
    y_prompt = xp.reshape(batch, seq, d)
    y_sample = xs.reshape(dec_b, dec_seq, d)
    return (y_prompt, y_sample) + tuple(jnp.stack(v) for v in outs)
```

```python
import functools
import math

import jax
import jax.numpy as jnp
from jax import lax
from jax.experimental import pallas as pl
from jax.experimental.pallas import tpu as pltpu

RMS_EPS = 1e-5
LN_EPS = 1e-5
CONV_WIDTH = 31
CONV_HALO = 32
N_HEADS = 8
HEAD_DIM = 64
V_DIM = 128
HEAD_SHIFT = 6
TOP_K = 4
SWIGLU_ALPHA = 1.702
SWIGLU_LIMIT = 7.0
DECODE_PAGES_PER_STEP = 8
MOE_ROW_BLOCK = 512
MOE_ROW_SUBBLOCK = 256
NEG_BIG = -1e30

V7X_VMEM_LIMIT_BYTES = 56 * 1024 * 1024
V7X_LANES = 128
BF16 = jnp.bfloat16
F32 = jnp.float32


def _params(n_axes):
    return pltpu.CompilerParams(dimension_semantics=("arbitrary",) * n_axes,
                                vmem_limit_bytes=V7X_VMEM_LIMIT_BYTES)


def _rms(x, g):
    return x * lax.rsqrt(jnp.mean(x * x, axis=-1, keepdims=True) + RMS_EPS) * g


def _lam_value(lq1, lk1, lq2, lk2, lam_init):
    s1 = jnp.sum(lq1 * lk1, axis=-1, keepdims=True)
    s2 = jnp.sum(lq2 * lk2, axis=-1, keepdims=True)
    return jnp.exp(s1) - jnp.exp(s2) + lam_init


def _inproj_kernel(x_ref, g_ref, w_ref, z_ref, h_sc):
    @pl.when(pl.program_id(1) == 0)
    def _():
        h_sc[...] = _rms(x_ref[...], g_ref[...]).astype(BF16)

    z_ref[...] = jnp.dot(h_sc[...], w_ref[...], preferred_element_type=F32)


def _inproj(x, g, w_bf16, tm, tn):
    t, d = x.shape
    n = w_bf16.shape[1]
    return pl.pallas_call(
        _inproj_kernel,
        grid=(t // tm, n // tn),
        in_specs=[pl.BlockSpec((tm, d), lambda i, j: (i, 0)),
                  pl.BlockSpec((1, d), lambda i, j: (0, 0)),
                  pl.BlockSpec((d, tn), lambda i, j: (0, j))],
        out_specs=pl.BlockSpec((tm, tn), lambda i, j: (i, j)),
        out_shape=jax.ShapeDtypeStruct((t, n), F32),
        scratch_shapes=[pltpu.VMEM((tm, d), BF16)],
        compiler_params=_params(2),
        name="inproj",
    )(x, g.reshape(1, d), w_bf16)


def _ln_silu(y, lng, lnb):
    mu = jnp.mean(y, axis=-1, keepdims=True)
    yc = y - mu
    var = jnp.mean(yc * yc, axis=-1, keepdims=True)
    yn = yc * lax.rsqrt(var + LN_EPS) * lng + lnb
    return yn * jax.nn.sigmoid(yn)


def _conv_prompt_kernel(a_ref, g_ref, wdw_ref, bdw_ref, lng_ref, lnb_ref, u_ref, c_ref, win_sc, *, rows):
    ts = a_ref.shape[0]

    @pl.when(pl.program_id(1) == 0)
    def _():
        win_sc[0:CONV_HALO, :] = jnp.zeros((CONV_HALO, win_sc.shape[1]), F32)

    u = a_ref[...] * jax.nn.sigmoid(g_ref[...])
    u_ref[...] = u
    win_sc[CONV_HALO:CONV_HALO + ts, :] = u
    off = CONV_HALO - (CONV_WIDTH - 1)
    for r0 in range(0, ts, rows):
        acc = jnp.zeros((rows, u.shape[1]), F32)
        for j in range(CONV_WIDTH):
            acc = acc + wdw_ref[j:j + 1, :] * win_sc[r0 + off + j:r0 + off + j + rows, :]
        y = acc + bdw_ref[...]
        c_ref[r0:r0 + rows, :] = _ln_silu(y, lng_ref[...], lnb_ref[...]).astype(BF16)
    win_sc[0:CONV_HALO, :] = win_sc[ts:ts + CONV_HALO, :]


def _conv_prompt(z, batch, seq, dc, w_dw, b_dw, ln_g, ln_b, ts):
    t = z.shape[0]
    nt = seq // ts
    vec = lambda v: v.reshape(1, dc)
    const = lambda shape: pl.BlockSpec(shape, lambda b, i: (0, 0))
    return pl.pallas_call(
        functools.partial(_conv_prompt_kernel, rows=32),
        grid=(batch, nt),
        in_specs=[pl.BlockSpec((ts, dc), lambda b, i: (b * nt + i, 0)),
                  pl.BlockSpec((ts, dc), lambda b, i: (b * nt + i, 1)),
                  const((CONV_WIDTH, dc)), const((1, dc)), const((1, dc)), const((1, dc))],
        out_specs=[pl.BlockSpec((ts, dc), lambda b, i: (b * nt + i, 0)),
                   pl.BlockSpec((ts, dc), lambda b, i: (b * nt + i, 0))],
        out_shape=[jax.ShapeDtypeStruct((t, dc), F32), jax.ShapeDtypeStruct((t, dc), BF16)],
        scratch_shapes=[pltpu.VMEM((ts + CONV_HALO, dc), F32)],
        compiler_params=_params(2),
        name="conv_prompt",
    )(z, z, w_dw, vec(b_dw), vec(ln_g), vec(ln_b))


def _conv_sample_kernel(a_ref, g_ref, st_ref, wdw_ref, bdw_ref, lng_ref, lnb_ref, u_ref, c_ref):
    nb = a_ref.shape[0]
    hist = CONV_WIDTH - 1
    u = a_ref[...] * jax.nn.sigmoid(g_ref[...])
    u_ref[...] = u

    def body(b, carry):
        past = jnp.sum(st_ref[b] * wdw_ref[0:hist, :], axis=0, keepdims=True)
        ub = u_ref[pl.ds(b, 1), :]
        y = past + ub * wdw_ref[hist:hist + 1, :] + bdw_ref[...]
        c_ref[pl.ds(b, 1), :] = _ln_silu(y, lng_ref[...], lnb_ref[...])
        return carry

    lax.fori_loop(0, nb, body, 0)


def _conv_sample(z, state, dc, w_dw, b_dw, ln_g, ln_b):
    nb = z.shape[0]
    vec = lambda v: v.reshape(1, dc)
    const = lambda shape: pl.BlockSpec(shape, lambda i: (0,) * len(shape))
    return pl.pallas_call(
        _conv_sample_kernel,
        grid=(1,),
        in_specs=[pl.BlockSpec((nb, dc), lambda i: (0, 0)),
                  pl.BlockSpec((nb, dc), lambda i: (0, 1)),
                  const(state.shape), const((CONV_WIDTH, dc)), const((1, dc)), const((1, dc)), const((1, dc))],
        out_specs=[const((nb, dc)), const((nb, dc))],
        out_shape=[jax.ShapeDtypeStruct((nb, dc), F32), jax.ShapeDtypeStruct((nb, dc), F32)],
        compiler_params=_params(1),
        name="conv_sample",
    )(z, z, state, w_dw, vec(b_dw), vec(ln_g), vec(ln_b))


def _attn_prompt_kernel(slopes_ref, q_ref, k_ref, v_ref, lq1_ref, lk1_ref, lq2_ref, lk2_ref, g_ref, o_ref,
                        vt_sc, kb_sc, acc_sc, *, lam_init):
    tq = q_ref.shape[0]
    n_kv = vt_sc.shape[0]
    h = pl.program_id(1)
    qi = pl.program_id(2)
    slope = slopes_ref[h]
    lam = _lam_value(lq1_ref[...], lk1_ref[...], lq2_ref[...], lk2_ref[...], lam_init)

    @pl.when(qi == 0)
    def _():
        for jb in range(n_kv):
            vt_sc[jb] = v_ref[jb * tq:(jb + 1) * tq, :].T.astype(BF16)
        kb_sc[...] = slope * lax.broadcasted_iota(jnp.int32, kb_sc.shape, 0).astype(F32)

    q = (q_ref[...] * (HEAD_DIM ** -0.5)).astype(BF16)
    lane = lax.broadcasted_iota(jnp.int32, q.shape, 1)
    zero = jnp.zeros_like(q)
    qq = jnp.concatenate([jnp.where(lane < HEAD_DIM, q, zero), jnp.where(lane >= HEAD_DIM, q, zero)], axis=0)
    acc_sc[...] = jnp.zeros(acc_sc.shape, F32)

    def step(j, carry, diagonal):
        m, l = carry
        kb = k_ref[pl.ds(pl.multiple_of(j * tq, tq), tq), :].astype(BF16)
        s = lax.dot_general(kb, qq, (((1,), (1,)), ((), ())), preferred_element_type=F32)
        bias = kb_sc[...]
        s = s + jnp.concatenate([bias] * (2 * tq // bias.shape[1]), axis=1)
        if diagonal:
            r = lax.broadcasted_iota(jnp.int32, s.shape, 0)
            c = lax.broadcasted_iota(jnp.int32, s.shape, 1)
            c = jnp.where(c >= tq, c - tq, c)
            s = jnp.where(r <= c, s, -jnp.inf)
        off = slope * ((j - qi) * tq).astype(F32)
        m_new = jnp.maximum(m, jnp.max(s, axis=0, keepdims=True) + off)
        alpha = jnp.exp(m - m_new)
        p = jnp.exp(s - (m_new - off))
        l = alpha * l + jnp.sum(p, axis=0, keepdims=True)
        acc_sc[...] = alpha * acc_sc[...] + jnp.dot(vt_sc[j], p.astype(BF16), preferred_element_type=F32)
        return m_new, l

    init = (jnp.full((1, 2 * tq), NEG_BIG, F32), jnp.zeros((1, 2 * tq), F32))
    carry = lax.fori_loop(0, qi, lambda j, c: step(j, c, False), init)
    _, l = step(qi, carry, True)
    o = acc_sc[...] / l
    d = o[:, :tq] - lam * o[:, tq:]
    dn = d * lax.rsqrt(jnp.mean(d * d, axis=0, keepdims=True) + RMS_EPS)
    o_ref[...] = (dn.T * g_ref[...] * (1.0 - lam_init)).astype(BF16)


def _attn_prompt(z, batch, seq, col_q, col_k, col_v, lams, subln_g, lam_init, tq):
    t = z.shape[0]
    nq = seq // tq
    slopes = 2.0 ** (-8.0 * jnp.arange(1, N_HEADS + 1, dtype=F32) / N_HEADS)
    blk = 2 * HEAD_DIM
    lam_spec = pl.BlockSpec((1, HEAD_DIM), lambda b, h, i, s: (0, 0))
    grid_spec = pltpu.PrefetchScalarGridSpec(
        num_scalar_prefetch=1,
        grid=(batch, N_HEADS, nq),
        in_specs=[pl.BlockSpec((tq, blk), lambda b, h, i, s: (b * nq + i, col_q // blk + h)),
                  pl.BlockSpec((seq, blk), lambda b, h, i, s: (b, col_k // blk + h)),
                  pl.BlockSpec((seq, V_DIM), lambda b, h, i, s: (b, col_v // V_DIM + h)),
                  lam_spec, lam_spec, lam_spec, lam_spec,
                  pl.BlockSpec((1, V_DIM), lambda b, h, i, s: (0, 0))],
        out_specs=pl.BlockSpec((tq, V_DIM), lambda b, h, i, s: (b * nq + i, h)),
        scratch_shapes=[pltpu.VMEM((nq, V_DIM, tq), BF16), pltpu.VMEM((tq, V7X_LANES), F32),
                        pltpu.VMEM((V_DIM, 2 * tq), F32)],
    )
    return pl.pallas_call(
        functools.partial(_attn_prompt_kernel, lam_init=lam_init),
        grid_spec=grid_spec,
        out_shape=jax.ShapeDtypeStruct((t, N_HEADS * V_DIM), BF16),
        compiler_params=_params(3),
        name="attn_prompt",
    )(slopes, z, z, z, *[v.reshape(1, HEAD_DIM) for v in lams], subln_g.reshape(1, V_DIM))


def _attn_decode_kernel(pt_ref, q_ref, kn_ref, vn_ref, spread_ref, *refs, past_len, pages):
    kc_refs, vc_refs = refs[:pages], refs[pages:2 * pages]
    o_ref, m_sc, l_sc, acc_sc, v_sc = refs[2 * pages:]
    p = pl.program_id(1)
    n_steps = pl.num_programs(1)
    n_maps = 2 * N_HEADS
    keys = kc_refs[0].shape[2]

    @pl.when(p == 0)
    def _():
        m_sc[...] = jnp.full(m_sc.shape, NEG_BIG, F32)
        l_sc[...] = jnp.zeros(l_sc.shape, F32)
        acc_sc[...] = jnp.zeros(acc_sc.shape, F32)

    q_row = q_ref[...] * (HEAD_DIM ** -0.5)
    qrow_i = lax.broadcasted_iota(jnp.int32, (n_maps, n_maps * HEAD_DIM), 0)
    qcol_i = lax.broadcasted_iota(jnp.int32, (n_maps, n_maps * HEAD_DIM), 1)
    qbd = jnp.where((qcol_i >> HEAD_SHIFT) == qrow_i, q_row, 0.0).astype(BF16)
    row1 = lax.broadcasted_iota(jnp.int32, (n_maps, 1), 0)
    slope = jnp.exp2(-((row1 >> 1) + 1).astype(F32))

    s = jnp.concatenate(
        [jnp.dot(qbd, kc_ref[...].reshape(n_maps * HEAD_DIM, keys).astype(BF16), preferred_element_type=F32)
         for kc_ref in kc_refs], axis=1)
    k_pos = p * (pages * keys) + lax.broadcasted_iota(jnp.int32, s.shape, 1)
    s = s - slope * (past_len - k_pos).astype(F32)

    m_prev = m_sc[...]
    m_new = jnp.maximum(m_prev, jnp.max(s, axis=1, keepdims=True))
    alpha = jnp.exp(m_prev - m_new)
    pe = jnp.exp(s - m_new)
    l_sc[...] = alpha * l_sc[...] + jnp.sum(pe, axis=1, keepdims=True)
    m_sc[...] = m_new

    pb = pe.astype(BF16)
    stacked = jnp.concatenate([pb[:, i * keys:(i + 1) * keys] for i in range(pages)], axis=0)
    spread = jnp.dot(stacked, spread_ref[...], preferred_element_type=F32)
    srow = lax.broadcasted_iota(jnp.int32, spread.shape, 0)
    scol = lax.broadcasted_iota(jnp.int32, spread.shape, 1)
    own_head = (scol & (N_HEADS - 1)) == ((srow & (n_maps - 1)) >> 1)
    pc = jnp.where(own_head, spread, 0.0).astype(BF16)
    pc = jnp.concatenate([pc[i * n_maps:(i + 1) * n_maps, :] for i in range(pages)], axis=1)
    rows = keys * N_HEADS
    for i, vc_ref in enumerate(vc_refs):
        v_sc[i * rows:(i + 1) * rows, :] = vc_ref[...].reshape(rows, V_DIM).astype(BF16)
    acc_sc[...] = alpha * acc_sc[...] + jnp.dot(pc, v_sc[...], preferred_element_type=F32)

    @pl.when(p == n_steps - 1)
    def _():
        kn = kn_ref[...].astype(BF16).astype(F32)
        s_new = jnp.sum(qbd.astype(F32) * kn, axis=1, keepdims=True)
        m_old = m_sc[...]
        m_fin = jnp.maximum(m_old, s_new)
        a = jnp.exp(m_old - m_fin)
        p_new = jnp.exp(s_new - m_fin)
        l_fin = a * l_sc[...] + p_new
        vn = vn_ref[...].astype(BF16).astype(F32)
        acc = a * acc_sc[...] + p_new.astype(BF16).astype(F32) * vn
        o_ref[...] = acc / l_fin


def _attn_decode(q_s, kn_s, vn_dup, cache_kt, cache_v, layer, page_table, past_len):
    nb, n_pages = page_table.shape
    n_maps = 2 * N_HEADS
    page = cache_kt.shape[-1]
    pages = DECODE_PAGES_PER_STEP if n_pages % DECODE_PAGES_PER_STEP == 0 else 1
    spread = (jnp.arange(page)[:, None] == jnp.arange(page * N_HEADS)[None, :] // N_HEADS).astype(BF16)

    def page_spec(shape, i):
        return pl.BlockSpec((None, None) + shape,
                            lambda b, p, pt: (layer, pt[b * n_pages + p * pages + i], 0, 0, 0))

    grid_spec = pltpu.PrefetchScalarGridSpec(
        num_scalar_prefetch=1,
        grid=(nb, n_pages // pages),
        in_specs=[pl.BlockSpec((None, 1, n_maps * HEAD_DIM), lambda b, p, pt: (b, 0, 0)),
                  pl.BlockSpec((None, 1, n_maps * HEAD_DIM), lambda b, p, pt: (b, 0, 0)),
                  pl.BlockSpec((None, n_maps, V_DIM), lambda b, p, pt: (b, 0, 0)),
                  pl.BlockSpec((page, page * N_HEADS), lambda b, p, pt: (0, 0))]
                 + [page_spec((n_maps, HEAD_DIM, page), i) for i in range(pages)]
                 + [page_spec((page, N_HEADS, V_DIM), i) for i in range(pages)],
        out_specs=pl.BlockSpec((None, n_maps, V_DIM), lambda b, p, pt: (b, 0, 0)),
        scratch_shapes=[pltpu.VMEM((n_maps, 1), F32), pltpu.VMEM((n_maps, 1), F32),
                        pltpu.VMEM((n_maps, V_DIM), F32), pltpu.VMEM((pages * page * N_HEADS, V_DIM), BF16)],
    )
    return pl.pallas_call(
        functools.partial(_attn_decode_kernel, past_len=past_len, pages=pages),
        grid_spec=grid_spec,
        out_shape=jax.ShapeDtypeStruct((nb, n_maps, V_DIM), F32),
        compiler_params=_params(2),
        name="attn_decode",
    )(page_table.reshape(-1), q_s, kn_s, vn_dup, spread, *([cache_kt] * pages), *([cache_v] * pages))


def _diff_norm_kernel(o_ref, lq1_ref, lk1_ref, lq2_ref, lk2_ref, g_ref, out_ref, *, lam_init):
    lam = _lam_value(lq1_ref[...], lk1_ref[...], lq2_ref[...], lk2_ref[...], lam_init)
    o = o_ref[...]
    d = o[:, :V_DIM] - lam * o[:, V_DIM:]
    out_ref[...] = _rms(d, g_ref[...]) * (1.0 - lam_init)


def _diff_norm(o_pairs, lams, subln_g, lam_init):
    rows = o_pairs.shape[0]
    const = lambda shape: pl.BlockSpec(shape, lambda i: (0, 0))
    return pl.pallas_call(
        functools.partial(_diff_norm_kernel, lam_init=lam_init),
        grid=(1,),
        in_specs=[const(o_pairs.shape)] + [const((1, HEAD_DIM))] * 4 + [const((1, V_DIM))],
        out_specs=const((rows, V_DIM)),
        out_shape=jax.ShapeDtypeStruct((rows, V_DIM), F32),
        compiler_params=_params(1),
        name="diff_norm",
    )(o_pairs, *[v.reshape(1, HEAD_DIM) for v in lams], subln_g.reshape(1, V_DIM))


def _pack_bf16_pairs(x):
    n = x.shape[1] // 2
    bits = lax.bitcast_convert_type(x.astype(BF16).astype(F32), jnp.uint32)
    return (bits[:, :n] >> 16) | (bits[:, n:] & jnp.uint32(0xFFFF0000))


def _unpack_bf16_pairs(w):
    lo = lax.bitcast_convert_type(w << 16, F32)
    hi = lax.bitcast_convert_type(w & jnp.uint32(0xFFFF0000), F32)
    return jnp.concatenate([lo, hi], axis=1).astype(BF16)


def _merge_kernel(x_ref, c_ref, o_ref, gc0_ref, gc1_ref, ga0_ref, ga1_ref, wco_ref, bco_ref, wao_ref, wo_ref,
                  g2_ref, x1_ref, h2_ref):
    half = gc0_ref.shape[1]
    conv_out = jnp.dot(c_ref[...].astype(BF16), wco_ref[...], preferred_element_type=F32) + bco_ref[...]
    attn_out = jnp.dot(o_ref[...].astype(BF16), wao_ref[...], preferred_element_type=F32)
    mixed = []
    for n, (gc_ref, ga_ref) in enumerate(((gc0_ref, ga0_ref), (gc1_ref, ga1_ref))):
        sl = slice(n * half, (n + 1) * half)
        mixed.append((jax.nn.sigmoid(gc_ref[...]) * conv_out[:, sl]
                      + jax.nn.sigmoid(ga_ref[...]) * attn_out[:, sl]).astype(BF16))
    x1 = x_ref[...] + jnp.dot(jnp.concatenate(mixed, axis=1), wo_ref[...], preferred_element_type=F32)
    x1_ref[...] = x1
    h2_ref[...] = _pack_bf16_pairs(_rms(x1, g2_ref[...]))


def _merge(x, c_act, o_norm, z, col_gc, col_ga, wco, bco, wao, wo, g2, tm):
    t, d = x.shape
    dc, da = c_act.shape[1], o_norm.shape[1]
    half = d // 2
    row = lambda w: pl.BlockSpec((tm, w), lambda i: (i, 0))
    gate = lambda blk: pl.BlockSpec((tm, half), lambda i: (i, blk))
    weight = lambda shape: pl.BlockSpec(shape, lambda i: (0, 0), pipeline_mode=pl.Buffered(1))
    return pl.pallas_call(
        _merge_kernel,
        grid=(t // tm,),
        in_specs=[row(d), row(dc), row(da),
                  gate(col_gc // half), gate(col_gc // half + 1), gate(col_ga // half), gate(col_ga // half + 1),
                  weight((dc, d)), weight((1, d)), weight((da, d)), weight((d, d)), weight((1, d))],
        out_specs=[row(d), row(half)],
        out_shape=[jax.ShapeDtypeStruct((t, d), F32), jax.ShapeDtypeStruct((t, half), jnp.uint32)],
        compiler_params=_params(1),
        name="merge",
    )(x, c_act, o_norm, z, z, z, z, wco, bco.reshape(1, d), wao, wo, g2.reshape(1, d))


def _router_kernel(h_ref, wr_ref, br_ref, base_ref, tri_ref, ti_ref, tw_ref, rk_ref, cnt_ref, cnt_sc, *, n_tokens):
    i = pl.program_id(0)
    tb = h_ref.shape[0]
    n_exp = wr_ref.shape[0]

    @pl.when(i == 0)
    def _():
        cnt_sc[...] = base_ref[...]

    logits = lax.dot_general(wr_ref[...], _unpack_bf16_pairs(h_ref[...]), (((1,), (1,)), ((), ())),
                             preferred_element_type=F32) + br_ref[...]
    tok = i * tb + lax.broadcasted_iota(jnp.int32, (1, tb), 1)
    valid = tok < n_tokens
    eidx = lax.broadcasted_iota(jnp.int32, (n_exp, tb), 0)
    run = cnt_sc[...]
    x = logits
    vals = []
    for k in range(TOP_K):
        mx = jnp.max(x, axis=0, keepdims=True)
        sel = jnp.min(jnp.where(x == mx, eidx, n_exp), axis=0, keepdims=True)
        hit = eidx == sel
        x = jnp.where(hit, -jnp.inf, x)
        onehot = jnp.where(hit & valid, 1.0, 0.0)
        before = jnp.dot(onehot.astype(BF16), tri_ref[...], preferred_element_type=F32)
        rank = jnp.sum(onehot * (before + run), axis=0, keepdims=True)
        run = run + jnp.sum(onehot, axis=1, keepdims=True)
        vals.append(mx)
        ti_ref[k:k + 1, :] = sel
        rk_ref[k:k + 1, :] = rank.astype(jnp.int32)
    es = [jnp.exp(v - vals[0]) for v in vals]
    tot = es[0] + es[1] + es[2] + es[3]
    for k in range(TOP_K):
        tw_ref[k:k + 1, :] = es[k] / tot
    cnt_sc[...] = run
    cnt_ref[...] = run


def _router(h2, wr_t, b_router, base, tb):
    t, d = h2.shape
    n_exp = wr_t.shape[0]
    nb = pl.cdiv(t, tb)
    tri = (jnp.arange(tb)[:, None] < jnp.arange(tb)[None, :]).astype(BF16)
    const = lambda shape: pl.BlockSpec(shape, lambda i: (0, 0))
    tok = lambda dt: jax.ShapeDtypeStruct((TOP_K, t), dt)
    return pl.pallas_call(
        functools.partial(_router_kernel, n_tokens=t),
        grid=(nb,),
        in_specs=[pl.BlockSpec((tb, d), lambda i: (i, 0)), const(wr_t.shape), const((n_exp, 1)),
                  const((n_exp, 1)), const((tb, tb))],
        out_specs=[pl.BlockSpec((TOP_K, tb), lambda i: (0, i))] * 3 + [const((n_exp, 1))],
        out_shape=[tok(jnp.int32), tok(F32), tok(jnp.int32), jax.ShapeDtypeStruct((n_exp, 1), F32)],
        scratch_shapes=[pltpu.VMEM((n_exp, 1), F32)],
        compiler_params=_params(1),
        name="router",
    )(h2, wr_t, b_router.reshape(n_exp, 1), base, tri)


def _block_index_list(dest, tb):
    t = dest.shape[1]
    return dest.reshape(TOP_K, t // tb, tb).transpose(1, 0, 2).reshape(-1)


def _dispatch_kernel(dest_ref, h_ref, xs_in_ref, xs_ref, idx_sm, idx_sem, row_sem):
    del xs_in_ref
    i = pl.program_id(0)
    tb = h_ref.shape[0]
    n_idx = TOP_K * tb
    cp = pltpu.make_async_copy(dest_ref.at[pl.ds(pl.multiple_of(i * n_idx, n_idx), n_idx)], idx_sm, idx_sem)
    cp.start()
    cp.wait()

    def row_copy(t, k):
        d = idx_sm[k * tb + t]
        return pltpu.make_async_copy(h_ref.at[pl.ds(t, 1), :], xs_ref.at[pl.ds(d, 1), :], row_sem)

    def issue(t, carry):
        for k in range(TOP_K):
            row_copy(t, k).start()
        return carry

    def drain(t, carry):
        for k in range(TOP_K):
            row_copy(t, k).wait()
        return carry

    lax.fori_loop(0, tb, issue, 0)
    lax.fori_loop(0, tb, drain, 0)


def _dispatch(h2, dest, xs, tb):
    t, d = h2.shape
    return pl.pallas_call(
        _dispatch_kernel,
        grid=(t // tb,),
        in_specs=[pl.BlockSpec(memory_space=pl.ANY),
                  pl.BlockSpec((tb, d), lambda i: (i, 0)),
                  pl.BlockSpec(memory_space=pl.ANY)],
        out_specs=pl.BlockSpec(memory_space=pl.ANY),
        out_shape=jax.ShapeDtypeStruct(xs.shape, xs.dtype),
        scratch_shapes=[pltpu.SMEM((TOP_K * tb,), jnp.int32), pltpu.SemaphoreType.DMA(()),
                        pltpu.SemaphoreType.DMA(())],
        input_output_aliases={2: 0},
        compiler_params=_params(1),
        name="moe_dispatch",
    )(_block_index_list(dest, tb), h2, xs)


def _combine_kernel(dest_ref, x1_ref, w_ref, g_ref, ys_ref, y_ref, idx_sm, rows_sc, idx_sem, row_sem, *, normalize):
    i = pl.program_id(0)
    tb = x1_ref.shape[0]
    n_idx = TOP_K * tb
    cp = pltpu.make_async_copy(dest_ref.at[pl.ds(pl.multiple_of(i * n_idx, n_idx), n_idx)], idx_sm, idx_sem)
    cp.start()
    cp.wait()

    def row_copy(t, k):
        d = idx_sm[k * tb + t]
        return pltpu.make_async_copy(ys_ref.at[pl.ds(d, 1), :], rows_sc.at[k, pl.ds(t, 1), :], row_sem)

    def issue(t, carry):
        for k in range(TOP_K):
            row_copy(t, k).start()
        return carry

    def drain(t, carry):
        for k in range(TOP_K):
            row_copy(t, k).wait()
        return carry

    lax.fori_loop(0, tb, issue, 0)
    lax.fori_loop(0, tb, drain, 0)
    y = x1_ref[...]
    for k in range(TOP_K):
        y = y + w_ref[:, k:k + 1] * rows_sc[k]
    y_ref[...] = _rms(y, g_ref[...]) if normalize else y


def _combine(x1, dest, top_w, ys, final_g, normalize, tb):
    t, d = x1.shape
    return pl.pallas_call(
        functools.partial(_combine_kernel, normalize=normalize),
        grid=(t // tb,),
        in_specs=[pl.BlockSpec(memory_space=pl.ANY),
                  pl.BlockSpec((tb, d), lambda i: (i, 0)),
                  pl.BlockSpec((tb, TOP_K), lambda i: (i, 0)),
                  pl.BlockSpec((1, d), lambda i: (0, 0)),
                  pl.BlockSpec(memory_space=pl.ANY)],
        out_specs=pl.BlockSpec((tb, d), lambda i: (i, 0)),
        out_shape=jax.ShapeDtypeStruct((t, d), F32),
        scratch_shapes=[pltpu.SMEM((TOP_K * tb,), jnp.int32), pltpu.VMEM((TOP_K, tb, d), F32),
                        pltpu.SemaphoreType.DMA(()), pltpu.SemaphoreType.DMA(())],
        compiler_params=_params(1),
        name="moe_combine",
    )(_block_index_list(dest, tb), x1, top_w.T, final_g.reshape(1, d), ys)


def _group_is_new(blk_e_ref, i, ic):
    return (i == 0) | (blk_e_ref[ic] != blk_e_ref[jnp.maximum(ic - 1, 0)])


def _block_rows(nact_ref, valid_ref, i):
    nact = nact_ref[0]
    ic = jnp.minimum(i, nact - 1)
    return ic, jnp.where(i < nact, valid_ref[ic], 0)


def _moe_gu_kernel(blk_e_ref, nact_ref, valid_ref, x_ref, wg_ref, wu_ref, bg_ref, bu_ref, h_ref, wg_sc, wu_sc,
                   *, sub):
    i = pl.program_id(1)
    ic, n_valid = _block_rows(nact_ref, valid_ref, i)

    @pl.when((n_valid > 0) & _group_is_new(blk_e_ref, i, ic))
    def _():
        wg_sc[...] = wg_ref[...].astype(BF16)
        wu_sc[...] = wu_ref[...].astype(BF16)

    for r0 in range(0, x_ref.shape[0], sub):
        @pl.when(n_valid > r0)
        def _(r0=r0):
            x = _unpack_bf16_pairs(x_ref[r0:r0 + sub, :])
            gt = jnp.minimum(jnp.dot(x, wg_sc[...], preferred_element_type=F32) + bg_ref[...], SWIGLU_LIMIT)
            up = jnp.clip(jnp.dot(x, wu_sc[...], preferred_element_type=F32) + bu_ref[...],
                          -SWIGLU_LIMIT, SWIGLU_LIMIT)
            h_ref[r0:r0 + sub, :] = (gt * jax.nn.sigmoid(SWIGLU_ALPHA * gt) * (up + 1.0)).astype(BF16)

        @pl.when(n_valid <= r0)
        def _(r0=r0):
            h_ref[r0:r0 + sub, :] = jnp.zeros((sub, h_ref.shape[1]), h_ref.dtype)


def _moe_down_kernel(blk_e_ref, nact_ref, valid_ref, h_ref, wd_ref, bd_ref, y_ref, wd_sc, *, sub):
    i = pl.program_id(1)
    ic, n_valid = _block_rows(nact_ref, valid_ref, i)

    @pl.when((n_valid > 0) & _group_is_new(blk_e_ref, i, ic))
    def _():
        wd_sc[...] = wd_ref[...].astype(BF16)

    for r0 in range(0, h_ref.shape[0], sub):
        @pl.when(n_valid > r0)
        def _(r0=r0):
            y_ref[r0:r0 + sub, :] = (jnp.dot(h_ref[r0:r0 + sub, :], wd_sc[...], preferred_element_type=F32)
                                     + bd_ref[...])

        @pl.when(n_valid <= r0)
        def _(r0=r0):
            y_ref[r0:r0 + sub, :] = jnp.zeros((sub, y_ref.shape[1]), y_ref.dtype)


def _moe_experts(xs, blk_e, nact, blk_valid, layer, wg, bg, wu, bu, wd, bd, tmb, sub, tn):
    rows, d_packed = xs.shape
    d = 2 * d_packed
    _, n_exp, _, de = wg.shape
    nblk = rows // tmb
    clamp = lambda i, na: jnp.minimum(i, na[0] - 1)
    lhs = lambda w: pl.BlockSpec((tmb, w), lambda n, i, be, na, nv: (clamp(i, na), 0))
    wspec = lambda k: pl.BlockSpec((None, None, k, tn), lambda n, i, be, na, nv: (layer, be[clamp(i, na)], 0, n))
    bspec = pl.BlockSpec((None, 1, tn), lambda n, i, be, na, nv: (be[clamp(i, na)], 0, n))
    out = pl.BlockSpec((tmb, tn), lambda n, i, be, na, nv: (i, n))
    bg, bu, bd = bg[layer], bu[layer], bd[layer]
    hidden = pl.pallas_call(
        functools.partial(_moe_gu_kernel, sub=sub),
        grid_spec=pltpu.PrefetchScalarGridSpec(
            num_scalar_prefetch=3, grid=(de // tn, nblk),
            in_specs=[lhs(d_packed), wspec(d), wspec(d), bspec, bspec], out_specs=out,
            scratch_shapes=[pltpu.VMEM((d, tn), BF16), pltpu.VMEM((d, tn), BF16)]),
        out_shape=jax.ShapeDtypeStruct((rows, de), BF16),
        compiler_params=_params(2),
        name="moe_gate_up",
    )(blk_e, nact, blk_valid, xs, wg, wu, bg.reshape(n_exp, 1, de), bu.reshape(n_exp, 1, de))
    return pl.pallas_call(
        functools.partial(_moe_down_kernel, sub=sub),
        grid_spec=pltpu.PrefetchScalarGridSpec(
            num_scalar_prefetch=3, grid=(d // tn, nblk),
            in_specs=[lhs(de), wspec(de), bspec], out_specs=out,
            scratch_shapes=[pltpu.VMEM((de, tn), BF16)]),
        out_shape=jax.ShapeDtypeStruct((rows, d), F32),
        compiler_params=_params(2),
        name="moe_down",
    )(blk_e, nact, blk_valid, hidden, wd, bd.reshape(n_exp, 1, d))


def _tile(n, pref):
    return pref if n % pref == 0 else n


def _token_mixers(x, is_prompt, batch, seq, lw, lam_init, sample_state):
    (norm1_g, w_in, w_dw, b_dw, conv_ln_g, conv_ln_b, w_conv_out, b_conv_out, lams, subln_g, w_attn_out, w_o,
     norm2_g) = lw
    t, d = x.shape
    dc = w_dw.shape[1]
    da = N_HEADS * V_DIM
    col_q, col_k, col_v = 2 * dc, 2 * dc + da, 2 * dc + 2 * da
    col_gc, col_ga = 2 * dc + 3 * da, 2 * dc + 3 * da + d
    z = _inproj(x, norm1_g, w_in, _tile(t, 512), 1536)
    k_new = z[:, col_k:col_v].reshape(batch, seq, 2 * N_HEADS, HEAD_DIM)
    v_new = z[:, col_v:col_gc].reshape(batch, seq, N_HEADS, V_DIM)
    if is_prompt:
        u, c_act = _conv_prompt(z, batch, seq, dc, w_dw, b_dw, conv_ln_g, conv_ln_b, _tile(seq, 256))
        conv_new = u.reshape(batch, seq, dc)[:, seq - (CONV_WIDTH - 1):]
        o_norm = _attn_prompt(z, batch, seq, col_q, col_k, col_v, lams, subln_g, lam_init, _tile(seq, 512))
    else:
        state_conv, cache_k, cache_v, layer, page_table, past_len = sample_state
        u, c_act = _conv_sample(z, state_conv, dc, w_dw, b_dw, conv_ln_g, conv_ln_b)
        conv_new = jnp.concatenate([state_conv[:, 1:], u[:, None, :]], axis=1)
        vn_dup = jnp.repeat(v_new.reshape(t, N_HEADS, V_DIM), 2, axis=1)
        cache_kt = jnp.transpose(cache_k, (0, 1, 3, 4, 2))
        o_maps = _attn_decode(z[:, col_q:col_k].reshape(t, 1, da), z[:, col_k:col_v].reshape(t, 1, da), vn_dup,
                              cache_kt, cache_v, layer, page_table, past_len)
        o_norm = _diff_norm(o_maps.reshape(t * N_HEADS, 2 * V_DIM), lams, subln_g, lam_init).reshape(t, da)
    x1, h2 = _merge(x, c_act, o_norm, z, col_gc, col_ga, w_conv_out, b_conv_out, w_attn_out, w_o, norm2_g,
                    _tile(t, 256))
    return x1, h2, k_new, v_new, conv_new


def _moe(groups, w_router, b_router, layer, wg, bg, wu, bu, wd, bd, final_g, normalize):
    _, n_exp, d, _ = wg.shape
    tmb, sub = MOE_ROW_BLOCK, MOE_ROW_SUBBLOCK
    wr_t = w_router.T.astype(BF16)
    base = jnp.zeros((n_exp, 1), F32)
    routed = []
    for x1, h2 in groups:
        top_i, top_w, rank, base = _router(h2, wr_t, b_router, base, _tile(h2.shape[0], 512))
        routed.append((top_i, top_w, rank))
    counts = base[:, 0].astype(jnp.int32)
    padded = (counts + tmb - 1) // tmb * tmb
    pend = jnp.cumsum(padded)
    pstart = pend - padded
    n_assign = sum(h2.shape[0] for _, h2 in groups) * TOP_K
    nblk = pl.cdiv(n_assign, tmb) + n_exp
    nact = (pend[-1] // tmb).astype(jnp.int32).reshape(1)
    blk_row = jnp.arange(nblk, dtype=jnp.int32) * tmb
    blk_e = jnp.minimum(jnp.sum(pend[None, :] <= blk_row[:, None], axis=1), n_exp - 1).astype(jnp.int32)
    experts = jnp.arange(n_exp, dtype=jnp.int32)
    blk_last = jnp.sum(jnp.where(blk_e[:, None] == experts[None, :], (pstart + counts)[None, :], 0), axis=1)
    blk_valid = jnp.clip(blk_last - blk_row, 0, tmb).astype(jnp.int32)
    xs = jnp.zeros((nblk * tmb, d // 2), jnp.uint32)
    dests = []
    for (x1, h2), (top_i, top_w, rank) in zip(groups, routed):
        first = jnp.sum(jnp.where(top_i[None] == experts[:, None, None], pstart[:, None, None], 0), axis=0)
        dest = first + rank
        dests.append(dest)
        xs = _dispatch(h2, dest, xs, _tile(h2.shape[0], 256))
    ys = _moe_experts(xs, blk_e, nact, blk_valid, layer, wg, bg, wu, bu, wd, bd, tmb, sub, 512)
    outs = []
    for (x1, h2), (top_i, top_w, rank), dest in zip(groups, routed, dests):
        outs.append(_combine(x1, dest, top_w, ys, final_g, normalize, _tile(x1.shape[0], 256)))
    return outs


def kernel(x_prompt, x_sample, cache_k, cache_v, state_conv, page_table, norm1_g, w_in, w_dw, b_dw, conv_ln_g, conv_ln_b, w_conv_out, b_conv_out, lambda_q1, lambda_k1, lambda_q2, lambda_k2, subln_g, w_attn_out, w_o, norm2_g, w_router, b_router, w_exp_gate, b_exp_gate, w_exp_up, b_exp_up, w_exp_down, b_exp_down, final_norm_g):
    depth = w_in.shape[0]
    batch, seq, d = x_prompt.shape
    dec_b, dec_seq, _ = x_sample.shape
    assert dec_seq == 1
    past_len = page_table.shape[1] * cache_k.shape[2]
    xp = x_prompt.reshape(batch * seq, d)
    xs = x_sample.reshape(dec_b * dec_seq, d)
    outs = [[] for _ in range(6)]
    for l in range(depth):
        lam_init = 0.8 - 0.6 * math.exp(-0.3 * l)
        lams = (lambda_q1[l], lambda_k1[l], lambda_q2[l], lambda_k2[l])
        lw = (norm1_g[l], w_in[l].astype(BF16), w_dw[l], b_dw[l], conv_ln_g[l], conv_ln_b[l],
              w_conv_out[l].astype(BF16), b_conv_out[l], lams, subln_g[l], w_attn_out[l].astype(BF16),
              w_o[l].astype(BF16), norm2_g[l])
        x1p, h2p, kp, vp, cp = _token_mixers(xp, True, batch, seq, lw, lam_init, None)
        x1s, h2s, ks, vs, cs = _token_mixers(xs, False, dec_b, dec_seq, lw, lam_init,
                                             (state_conv[l], cache_k, cache_v, l, page_table, past_len))
        xp, xs = _moe([(x1p, h2p), (x1s, h2s)], w_router[l], b_router[l], l, w_exp_gate, b_exp_gate,
                      w_exp_up, b_exp_up, w_exp_down, b_exp_down, final_norm_g, l == depth - 1)
        for lst, v in zip(outs, (kp, vp, cp, ks, vs, cs)):
            lst.append(v)
    y_prompt = xp.reshape(batch, seq, d)
    y_sample = xs.reshape(dec_b, dec_seq, d)
    return (y_prompt, y_sample) + tuple(jnp.stack(v) for v in outs)
```

```python
import functools
import math

import jax
import jax.numpy as jnp
from jax import lax
from jax.experimental import pallas as pl
from jax.experimental.pallas import tpu as pltpu

RMS_EPS = 1e-5
LN_EPS = 1e-5
CONV_WIDTH = 31
CONV_HALO = 32
N_HEADS = 8
HEAD_DIM = 64
V_DIM = 128
HEAD_SHIFT = 6
TOP_K = 4
SWIGLU_ALPHA = 1.702
SWIGLU_LIMIT = 7.0
DECODE_PAGES_PER_STEP = 8
MOE_ROW_BLOCK = 512
MOE_ROW_SUBBLOCK = 256
MOE_UP_COLS = 512
MOE_DOWN_COLS = 1024
NEG_BIG = -1e30

V7X_VMEM_LIMIT_BYTES = 56 * 1024 * 1024
V7X_LANES = 128
BF16 = jnp.bfloat16
F32 = jnp.float32


def _params(n_axes):
    return pltpu.CompilerParams(dimension_semantics=("arbitrary",) * n_axes,
                                vmem_limit_bytes=V7X_VMEM_LIMIT_BYTES)


def _rms(x, g):
    return x * lax.rsqrt(jnp.mean(x * x, axis=-1, keepdims=True) + RMS_EPS) * g


def _lam_value(lq1, lk1, lq2, lk2, lam_init):
    s1 = jnp.sum(lq1 * lk1, axis=-1, keepdims=True)
    s2 = jnp.sum(lq2 * lk2, axis=-1, keepdims=True)
    return jnp.exp(s1) - jnp.exp(s2) + lam_init


def _inproj_kernel(x_ref, g_ref, w_ref, z_ref, h_sc):
    @pl.when(pl.program_id(1) == 0)
    def _():
        h_sc[...] = _rms(x_ref[...], g_ref[...]).astype(BF16)

    z_ref[...] = jnp.dot(h_sc[...], w_ref[...], preferred_element_type=F32)


def _inproj(x, g, w_bf16, tm, tn):
    t, d = x.shape
    n = w_bf16.shape[1]
    return pl.pallas_call(
        _inproj_kernel,
        grid=(t // tm, n // tn),
        in_specs=[pl.BlockSpec((tm, d), lambda i, j: (i, 0)),
                  pl.BlockSpec((1, d), lambda i, j: (0, 0)),
                  pl.BlockSpec((d, tn), lambda i, j: (0, j))],
        out_specs=pl.BlockSpec((tm, tn), lambda i, j: (i, j)),
        out_shape=jax.ShapeDtypeStruct((t, n), F32),
        scratch_shapes=[pltpu.VMEM((tm, d), BF16)],
        compiler_params=_params(2),
        name="inproj",
    )(x, g.reshape(1, d), w_bf16)


def _ln_silu(y, lng, lnb):
    mu = jnp.mean(y, axis=-1, keepdims=True)
    yc = y - mu
    var = jnp.mean(yc * yc, axis=-1, keepdims=True)
    yn = yc * lax.rsqrt(var + LN_EPS) * lng + lnb
    return yn * jax.nn.sigmoid(yn)


def _conv_prompt_kernel(a_ref, g_ref, wdw_ref, bdw_ref, lng_ref, lnb_ref, u_ref, c_ref, win_sc, *, rows):
    ts = a_ref.shape[0]

    @pl.when(pl.program_id(1) == 0)
    def _():
        win_sc[0:CONV_HALO, :] = jnp.zeros((CONV_HALO, win_sc.shape[1]), F32)

    u = a_ref[...] * jax.nn.sigmoid(g_ref[...])
    u_ref[...] = u
    win_sc[CONV_HALO:CONV_HALO + ts, :] = u
    off = CONV_HALO - (CONV_WIDTH - 1)
    for r0 in range(0, ts, rows):
        acc = jnp.zeros((rows, u.shape[1]), F32)
        for j in range(CONV_WIDTH):
            acc = acc + wdw_ref[j:j + 1, :] * win_sc[r0 + off + j:r0 + off + j + rows, :]
        y = acc + bdw_ref[...]
        c_ref[r0:r0 + rows, :] = _ln_silu(y, lng_ref[...], lnb_ref[...]).astype(BF16)
    win_sc[0:CONV_HALO, :] = win_sc[ts:ts + CONV_HALO, :]


def _conv_prompt(z, batch, seq, dc, w_dw, b_dw, ln_g, ln_b, ts):
    t = z.shape[0]
    nt = seq // ts
    vec = lambda v: v.reshape(1, dc)
    const = lambda shape: pl.BlockSpec(shape, lambda b, i: (0, 0))
    return pl.pallas_call(
        functools.partial(_conv_prompt_kernel, rows=32),
        grid=(batch, nt),
        in_specs=[pl.BlockSpec((ts, dc), lambda b, i: (b * nt + i, 0)),
                  pl.BlockSpec((ts, dc), lambda b, i: (b * nt + i, 1)),
                  const((CONV_WIDTH, dc)), const((1, dc)), const((1, dc)), const((1, dc))],
        out_specs=[pl.BlockSpec((ts, dc), lambda b, i: (b * nt + i, 0)),
                   pl.BlockSpec((ts, dc), lambda b, i: (b * nt + i, 0))],
        out_shape=[jax.ShapeDtypeStruct((t, dc), F32), jax.ShapeDtypeStruct((t, dc), BF16)],
        scratch_shapes=[pltpu.VMEM((ts + CONV_HALO, dc), F32)],
        compiler_params=_params(2),
        name="conv_prompt",
    )(z, z, w_dw, vec(b_dw), vec(ln_g), vec(ln_b))


def _conv_sample_kernel(a_ref, g_ref, st_ref, wdw_ref, bdw_ref, lng_ref, lnb_ref, u_ref, c_ref):
    nb = a_ref.shape[0]
    hist = CONV_WIDTH - 1
    u = a_ref[...] * jax.nn.sigmoid(g_ref[...])
    u_ref[...] = u

    def body(b, carry):
        past = jnp.sum(st_ref[b] * wdw_ref[0:hist, :], axis=0, keepdims=True)
        ub = u_ref[pl.ds(b, 1), :]
        y = past + ub * wdw_ref[hist:hist + 1, :] + bdw_ref[...]
        c_ref[pl.ds(b, 1), :] = _ln_silu(y, lng_ref[...], lnb_ref[...])
        return carry

    lax.fori_loop(0, nb, body, 0)


def _conv_sample(z, state, dc, w_dw, b_dw, ln_g, ln_b):
    nb = z.shape[0]
    vec = lambda v: v.reshape(1, dc)
    const = lambda shape: pl.BlockSpec(shape, lambda i: (0,) * len(shape))
    return pl.pallas_call(
        _conv_sample_kernel,
        grid=(1,),
        in_specs=[pl.BlockSpec((nb, dc), lambda i: (0, 0)),
                  pl.BlockSpec((nb, dc), lambda i: (0, 1)),
                  const(state.shape), const((CONV_WIDTH, dc)), const((1, dc)), const((1, dc)), const((1, dc))],
        out_specs=[const((nb, dc)), const((nb, dc))],
        out_shape=[jax.ShapeDtypeStruct((nb, dc), F32), jax.ShapeDtypeStruct((nb, dc), F32)],
        compiler_params=_params(1),
        name="conv_sample",
    )(z, z, state, w_dw, vec(b_dw), vec(ln_g), vec(ln_b))


def _attn_prompt_kernel(slopes_ref, q_ref, k_ref, v_ref, lq1_ref, lk1_ref, lq2_ref, lk2_ref, g_ref, o_ref,
                        vt_sc, kb_sc, acc_sc, *, lam_init):
    tq = q_ref.shape[0]
    n_kv = vt_sc.shape[0]
    h = pl.program_id(1)
    qi = pl.program_id(2)
    slope = slopes_ref[h]
    lam = _lam_value(lq1_ref[...], lk1_ref[...], lq2_ref[...], lk2_ref[...], lam_init)

    @pl.when(qi == 0)
    def _():
        for jb in range(n_kv):
            vt_sc[jb] = v_ref[jb * tq:(jb + 1) * tq, :].T.astype(BF16)
        kb_sc[...] = slope * lax.broadcasted_iota(jnp.int32, kb_sc.shape, 0).astype(F32)

    q = (q_ref[...] * (HEAD_DIM ** -0.5)).astype(BF16)
    lane = lax.broadcasted_iota(jnp.int32, q.shape, 1)
    zero = jnp.zeros_like(q)
    qq = jnp.concatenate([jnp.where(lane < HEAD_DIM, q, zero), jnp.where(lane >= HEAD_DIM, q, zero)], axis=0)
    acc_sc[...] = jnp.zeros(acc_sc.shape, F32)

    def step(j, carry, diagonal):
        m, l = carry
        kb = k_ref[pl.ds(pl.multiple_of(j * tq, tq), tq), :].astype(BF16)
        s = lax.dot_general(kb, qq, (((1,), (1,)), ((), ())), preferred_element_type=F32)
        bias = kb_sc[...]
        s = s + jnp.concatenate([bias] * (2 * tq // bias.shape[1]), axis=1)
        if diagonal:
            r = lax.broadcasted_iota(jnp.int32, s.shape, 0)
            c = lax.broadcasted_iota(jnp.int32, s.shape, 1)
            c = jnp.where(c >= tq, c - tq, c)
            s = jnp.where(r <= c, s, -jnp.inf)
        off = slope * ((j - qi) * tq).astype(F32)
        m_new = jnp.maximum(m, jnp.max(s, axis=0, keepdims=True) + off)
        alpha = jnp.exp(m - m_new)
        p = jnp.exp(s - (m_new - off))
        l = alpha * l + jnp.sum(p, axis=0, keepdims=True)
        acc_sc[...] = alpha * acc_sc[...] + jnp.dot(vt_sc[j], p.astype(BF16), preferred_element_type=F32)
        return m_new, l

    init = (jnp.full((1, 2 * tq), NEG_BIG, F32), jnp.zeros((1, 2 * tq), F32))
    carry = lax.fori_loop(0, qi, lambda j, c: step(j, c, False), init)
    _, l = step(qi, carry, True)
    o = acc_sc[...] / l
    d = o[:, :tq] - lam * o[:, tq:]
    dn = d * lax.rsqrt(jnp.mean(d * d, axis=0, keepdims=True) + RMS_EPS)
    o_ref[...] = (dn.T * g_ref[...] * (1.0 - lam_init)).astype(BF16)


def _attn_prompt(z, batch, seq, col_q, col_k, col_v, lams, subln_g, lam_init, tq):
    t = z.shape[0]
    nq = seq // tq
    slopes = 2.0 ** (-8.0 * jnp.arange(1, N_HEADS + 1, dtype=F32) / N_HEADS)
    blk = 2 * HEAD_DIM
    lam_spec = pl.BlockSpec((1, HEAD_DIM), lambda b, h, i, s: (0, 0))
    grid_spec = pltpu.PrefetchScalarGridSpec(
        num_scalar_prefetch=1,
        grid=(batch, N_HEADS, nq),
        in_specs=[pl.BlockSpec((tq, blk), lambda b, h, i, s: (b * nq + i, col_q // blk + h)),
                  pl.BlockSpec((seq, blk), lambda b, h, i, s: (b, col_k // blk + h)),
                  pl.BlockSpec((seq, V_DIM), lambda b, h, i, s: (b, col_v // V_DIM + h)),
                  lam_spec, lam_spec, lam_spec, lam_spec,
                  pl.BlockSpec((1, V_DIM), lambda b, h, i, s: (0, 0))],
        out_specs=pl.BlockSpec((tq, V_DIM), lambda b, h, i, s: (b * nq + i, h)),
        scratch_shapes=[pltpu.VMEM((nq, V_DIM, tq), BF16), pltpu.VMEM((tq, V7X_LANES), F32),
                        pltpu.VMEM((V_DIM, 2 * tq), F32)],
    )
    return pl.pallas_call(
        functools.partial(_attn_prompt_kernel, lam_init=lam_init),
        grid_spec=grid_spec,
        out_shape=jax.ShapeDtypeStruct((t, N_HEADS * V_DIM), BF16),
        compiler_params=_params(3),
        name="attn_prompt",
    )(slopes, z, z, z, *[v.reshape(1, HEAD_DIM) for v in lams], subln_g.reshape(1, V_DIM))


def _attn_decode_kernel(pt_ref, q_ref, kn_ref, vn_ref, spread_ref, *refs, past_len, pages):
    kc_refs, vc_refs = refs[:pages], refs[pages:2 * pages]
    o_ref, m_sc, l_sc, acc_sc, v_sc = refs[2 * pages:]
    p = pl.program_id(1)
    n_steps = pl.num_programs(1)
    n_maps = 2 * N_HEADS
    keys = kc_refs[0].shape[2]

    @pl.when(p == 0)
    def _():
        m_sc[...] = jnp.full(m_sc.shape, NEG_BIG, F32)
        l_sc[...] = jnp.zeros(l_sc.shape, F32)
        acc_sc[...] = jnp.zeros(acc_sc.shape, F32)

    q_row = q_ref[...] * (HEAD_DIM ** -0.5)
    qrow_i = lax.broadcasted_iota(jnp.int32, (n_maps, n_maps * HEAD_DIM), 0)
    qcol_i = lax.broadcasted_iota(jnp.int32, (n_maps, n_maps * HEAD_DIM), 1)
    qbd = jnp.where((qcol_i >> HEAD_SHIFT) == qrow_i, q_row, 0.0).astype(BF16)
    row1 = lax.broadcasted_iota(jnp.int32, (n_maps, 1), 0)
    slope = jnp.exp2(-((row1 >> 1) + 1).astype(F32))

    s = jnp.concatenate(
        [jnp.dot(qbd, kc_ref[...].reshape(n_maps * HEAD_DIM, keys).astype(BF16), preferred_element_type=F32)
         for kc_ref in kc_refs], axis=1)
    k_pos = p * (pages * keys) + lax.broadcasted_iota(jnp.int32, s.shape, 1)
    s = s - slope * (past_len - k_pos).astype(F32)

    m_prev = m_sc[...]
    m_new = jnp.maximum(m_prev, jnp.max(s, axis=1, keepdims=True))
    alpha = jnp.exp(m_prev - m_new)
    pe = jnp.exp(s - m_new)
    l_sc[...] = alpha * l_sc[...] + jnp.sum(pe, axis=1, keepdims=True)
    m_sc[...] = m_new

    pb = pe.astype(BF16)
    stacked = jnp.concatenate([pb[:, i * keys:(i + 1) * keys] for i in range(pages)], axis=0)
    spread = jnp.dot(stacked, spread_ref[...], preferred_element_type=F32)
    srow = lax.broadcasted_iota(jnp.int32, spread.shape, 0)
    scol = lax.broadcasted_iota(jnp.int32, spread.shape, 1)
    own_head = (scol & (N_HEADS - 1)) == ((srow & (n_maps - 1)) >> 1)
    pc = jnp.where(own_head, spread, 0.0).astype(BF16)
    pc = jnp.concatenate([pc[i * n_maps:(i + 1) * n_maps, :] for i in range(pages)], axis=1)
    rows = keys * N_HEADS
    for i, vc_ref in enumerate(vc_refs):
        v_sc[i * rows:(i + 1) * rows, :] = vc_ref[...].reshape(rows, V_DIM).astype(BF16)
    acc_sc[...] = alpha * acc_sc[...] + jnp.dot(pc, v_sc[...], preferred_element_type=F32)

    @pl.when(p == n_steps - 1)
    def _():
        kn = kn_ref[...].astype(BF16).astype(F32)
        s_new = jnp.sum(qbd.astype(F32) * kn, axis=1, keepdims=True)
        m_old = m_sc[...]
        m_fin = jnp.maximum(m_old, s_new)
        a = jnp.exp(m_old - m_fin)
        p_new = jnp.exp(s_new - m_fin)
        l_fin = a * l_sc[...] + p_new
        vn = vn_ref[...].astype(BF16).astype(F32)
        acc = a * acc_sc[...] + p_new.astype(BF16).astype(F32) * vn
        o_ref[...] = acc / l_fin


def _attn_decode(q_s, kn_s, vn_dup, cache_kt, cache_v, layer, page_table, past_len):
    nb, n_pages = page_table.shape
    n_maps = 2 * N_HEADS
    page = cache_kt.shape[-1]
    pages = DECODE_PAGES_PER_STEP if n_pages % DECODE_PAGES_PER_STEP == 0 else 1
    spread = (jnp.arange(page)[:, None] == jnp.arange(page * N_HEADS)[None, :] // N_HEADS).astype(BF16)

    def page_spec(shape, i):
        return pl.BlockSpec((None, None) + shape,
                            lambda b, p, pt: (layer, pt[b * n_pages + p * pages + i], 0, 0, 0))

    grid_spec = pltpu.PrefetchScalarGridSpec(
        num_scalar_prefetch=1,
        grid=(nb, n_pages // pages),
        in_specs=[pl.BlockSpec((None, 1, n_maps * HEAD_DIM), lambda b, p, pt: (b, 0, 0)),
                  pl.BlockSpec((None, 1, n_maps * HEAD_DIM), lambda b, p, pt: (b, 0, 0)),
                  pl.BlockSpec((None, n_maps, V_DIM), lambda b, p, pt: (b, 0, 0)),
                  pl.BlockSpec((page, page * N_HEADS), lambda b, p, pt: (0, 0))]
                 + [page_spec((n_maps, HEAD_DIM, page), i) for i in range(pages)]
                 + [page_spec((page, N_HEADS, V_DIM), i) for i in range(pages)],
        out_specs=pl.BlockSpec((None, n_maps, V_DIM), lambda b, p, pt: (b, 0, 0)),
        scratch_shapes=[pltpu.VMEM((n_maps, 1), F32), pltpu.VMEM((n_maps, 1), F32),
                        pltpu.VMEM((n_maps, V_DIM), F32), pltpu.VMEM((pages * page * N_HEADS, V_DIM), BF16)],
    )
    return pl.pallas_call(
        functools.partial(_attn_decode_kernel, past_len=past_len, pages=pages),
        grid_spec=grid_spec,
        out_shape=jax.ShapeDtypeStruct((nb, n_maps, V_DIM), F32),
        compiler_params=_params(2),
        name="attn_decode",
    )(page_table.reshape(-1), q_s, kn_s, vn_dup, spread, *([cache_kt] * pages), *([cache_v] * pages))


def _diff_norm_kernel(o_ref, lq1_ref, lk1_ref, lq2_ref, lk2_ref, g_ref, out_ref, *, lam_init):
    lam = _lam_value(lq1_ref[...], lk1_ref[...], lq2_ref[...], lk2_ref[...], lam_init)
    o = o_ref[...]
    d = o[:, :V_DIM] - lam * o[:, V_DIM:]
    out_ref[...] = _rms(d, g_ref[...]) * (1.0 - lam_init)


def _diff_norm(o_pairs, lams, subln_g, lam_init):
    rows = o_pairs.shape[0]
    const = lambda shape: pl.BlockSpec(shape, lambda i: (0, 0))
    return pl.pallas_call(
        functools.partial(_diff_norm_kernel, lam_init=lam_init),
        grid=(1,),
        in_specs=[const(o_pairs.shape)] + [const((1, HEAD_DIM))] * 4 + [const((1, V_DIM))],
        out_specs=const((rows, V_DIM)),
        out_shape=jax.ShapeDtypeStruct((rows, V_DIM), F32),
        compiler_params=_params(1),
        name="diff_norm",
    )(o_pairs, *[v.reshape(1, HEAD_DIM) for v in lams], subln_g.reshape(1, V_DIM))


def _pack_bf16_pairs(x):
    n = x.shape[1] // 2
    bits = lax.bitcast_convert_type(x.astype(BF16).astype(F32), jnp.uint32)
    return (bits[:, :n] >> 16) | (bits[:, n:] & jnp.uint32(0xFFFF0000))


def _unpack_bf16_pairs(w):
    lo = lax.bitcast_convert_type(w << 16, F32)
    hi = lax.bitcast_convert_type(w & jnp.uint32(0xFFFF0000), F32)
    return jnp.concatenate([lo, hi], axis=1).astype(BF16)


def _merge_kernel(x_ref, c_ref, o_ref, gc0_ref, gc1_ref, ga0_ref, ga1_ref, wco_ref, bco_ref, wao_ref, wo_ref,
                  g2_ref, x1_ref, h2_ref):
    half = gc0_ref.shape[1]
    conv_out = jnp.dot(c_ref[...].astype(BF16), wco_ref[...], preferred_element_type=F32) + bco_ref[...]
    attn_out = jnp.dot(o_ref[...].astype(BF16), wao_ref[...], preferred_element_type=F32)
    mixed = []
    for n, (gc_ref, ga_ref) in enumerate(((gc0_ref, ga0_ref), (gc1_ref, ga1_ref))):
        sl = slice(n * half, (n + 1) * half)
        mixed.append((jax.nn.sigmoid(gc_ref[...]) * conv_out[:, sl]
                      + jax.nn.sigmoid(ga_ref[...]) * attn_out[:, sl]).astype(BF16))
    x1 = x_ref[...] + jnp.dot(jnp.concatenate(mixed, axis=1), wo_ref[...], preferred_element_type=F32)
    x1_ref[...] = x1
    h2_ref[...] = _pack_bf16_pairs(_rms(x1, g2_ref[...]))


def _merge(x, c_act, o_norm, z, col_gc, col_ga, wco, bco, wao, wo, g2, tm):
    t, d = x.shape
    dc, da = c_act.shape[1], o_norm.shape[1]
    half = d // 2
    row = lambda w: pl.BlockSpec((tm, w), lambda i: (i, 0))
    gate = lambda blk: pl.BlockSpec((tm, half), lambda i: (i, blk))
    weight = lambda shape: pl.BlockSpec(shape, lambda i: (0, 0), pipeline_mode=pl.Buffered(1))
    return pl.pallas_call(
        _merge_kernel,
        grid=(t // tm,),
        in_specs=[row(d), row(dc), row(da),
                  gate(col_gc // half), gate(col_gc // half + 1), gate(col_ga // half), gate(col_ga // half + 1),
                  weight((dc, d)), weight((1, d)), weight((da, d)), weight((d, d)), weight((1, d))],
        out_specs=[row(d), row(half)],
        out_shape=[jax.ShapeDtypeStruct((t, d), F32), jax.ShapeDtypeStruct((t, half), jnp.uint32)],
        compiler_params=_params(1),
        name="merge",
    )(x, c_act, o_norm, z, z, z, z, wco, bco.reshape(1, d), wao, wo, g2.reshape(1, d))


def _router_kernel(h_ref, wr_ref, br_ref, base_ref, tri_ref, ti_ref, tw_ref, rk_ref, cnt_ref, cnt_sc, *, n_tokens):
    i = pl.program_id(0)
    tb = h_ref.shape[0]
    n_exp = wr_ref.shape[0]

    @pl.when(i == 0)
    def _():
        cnt_sc[...] = base_ref[...]

    logits = lax.dot_general(wr_ref[...], _unpack_bf16_pairs(h_ref[...]), (((1,), (1,)), ((), ())),
                             preferred_element_type=F32) + br_ref[...]
    tok = i * tb + lax.broadcasted_iota(jnp.int32, (1, tb), 1)
    valid = tok < n_tokens
    eidx = lax.broadcasted_iota(jnp.int32, (n_exp, tb), 0)
    run = cnt_sc[...]
    x = logits
    vals = []
    for k in range(TOP_K):
        mx = jnp.max(x, axis=0, keepdims=True)
        sel = jnp.min(jnp.where(x == mx, eidx, n_exp), axis=0, keepdims=True)
        hit = eidx == sel
        x = jnp.where(hit, -jnp.inf, x)
        onehot = jnp.where(hit & valid, 1.0, 0.0)
        before = jnp.dot(onehot.astype(BF16), tri_ref[...], preferred_element_type=F32)
        rank = jnp.sum(onehot * (before + run), axis=0, keepdims=True)
        run = run + jnp.sum(onehot, axis=1, keepdims=True)
        vals.append(mx)
        ti_ref[k:k + 1, :] = sel
        rk_ref[k:k + 1, :] = rank.astype(jnp.int32)
    es = [jnp.exp(v - vals[0]) for v in vals]
    tot = es[0] + es[1] + es[2] + es[3]
    for k in range(TOP_K):
        tw_ref[k:k + 1, :] = es[k] / tot
    cnt_sc[...] = run
    cnt_ref[...] = run


def _router(h2, wr_t, b_router, base, tb):
    t, d = h2.shape
    n_exp = wr_t.shape[0]
    nb = pl.cdiv(t, tb)
    tri = (jnp.arange(tb)[:, None] < jnp.arange(tb)[None, :]).astype(BF16)
    const = lambda shape: pl.BlockSpec(shape, lambda i: (0, 0))
    tok = lambda dt: jax.ShapeDtypeStruct((TOP_K, t), dt)
    return pl.pallas_call(
        functools.partial(_router_kernel, n_tokens=t),
        grid=(nb,),
        in_specs=[pl.BlockSpec((tb, d), lambda i: (i, 0)), const(wr_t.shape), const((n_exp, 1)),
                  const((n_exp, 1)), const((tb, tb))],
        out_specs=[pl.BlockSpec((TOP_K, tb), lambda i: (0, i))] * 3 + [const((n_exp, 1))],
        out_shape=[tok(jnp.int32), tok(F32), tok(jnp.int32), jax.ShapeDtypeStruct((n_exp, 1), F32)],
        scratch_shapes=[pltpu.VMEM((n_exp, 1), F32)],
        compiler_params=_params(1),
        name="router",
    )(h2, wr_t, b_router.reshape(n_exp, 1), base, tri)


def _block_index_list(dest, tb):
    t = dest.shape[1]
    return dest.reshape(TOP_K, t // tb, tb).transpose(1, 0, 2).reshape(-1)


def _dispatch_kernel(dest_ref, h_ref, xs_in_ref, xs_ref, idx_sm, idx_sem, row_sem):
    del xs_in_ref
    i = pl.program_id(0)
    tb = h_ref.shape[0]
    n_idx = TOP_K * tb
    cp = pltpu.make_async_copy(dest_ref.at[pl.ds(pl.multiple_of(i * n_idx, n_idx), n_idx)], idx_sm, idx_sem)
    cp.start()
    cp.wait()

    def row_copy(t, k):
        d = idx_sm[k * tb + t]
        return pltpu.make_async_copy(h_ref.at[pl.ds(t, 1), :], xs_ref.at[pl.ds(d, 1), :], row_sem)

    def issue(t, carry):
        for k in range(TOP_K):
            row_copy(t, k).start(priority=k % 2)
        return carry

    def drain(t, carry):
        for k in range(TOP_K):
            row_copy(t, k).wait()
        return carry

    lax.fori_loop(0, tb, issue, 0)
    lax.fori_loop(0, tb, drain, 0)


def _dispatch(h2, dest, xs, tb):
    t, d = h2.shape
    return pl.pallas_call(
        _dispatch_kernel,
        grid=(t // tb,),
        in_specs=[pl.BlockSpec(memory_space=pl.ANY),
                  pl.BlockSpec((tb, d), lambda i: (i, 0)),
                  pl.BlockSpec(memory_space=pl.ANY)],
        out_specs=pl.BlockSpec(memory_space=pl.ANY),
        out_shape=jax.ShapeDtypeStruct(xs.shape, xs.dtype),
        scratch_shapes=[pltpu.SMEM((TOP_K * tb,), jnp.int32), pltpu.SemaphoreType.DMA(()),
                        pltpu.SemaphoreType.DMA(())],
        input_output_aliases={2: 0},
        compiler_params=_params(1),
        name="moe_dispatch",
    )(_block_index_list(dest, tb), h2, xs)


def _combine_kernel(dest_ref, x1_ref, w_ref, g_ref, ys_ref, y_ref, idx_sm, rows_sc, idx_sem, row_sem, *, normalize):
    i = pl.program_id(0)
    tb = x1_ref.shape[0]
    n_idx = TOP_K * tb
    cp = pltpu.make_async_copy(dest_ref.at[pl.ds(pl.multiple_of(i * n_idx, n_idx), n_idx)], idx_sm, idx_sem)
    cp.start()
    cp.wait()

    def row_copy(t, k):
        d = idx_sm[k * tb + t]
        return pltpu.make_async_copy(ys_ref.at[pl.ds(d, 1), :], rows_sc.at[k, pl.ds(t, 1), :], row_sem)

    def issue(t, carry):
        for k in range(TOP_K):
            row_copy(t, k).start(priority=k % 2)
        return carry

    def drain(t, carry):
        for k in range(TOP_K):
            row_copy(t, k).wait()
        return carry

    lax.fori_loop(0, tb, issue, 0)
    lax.fori_loop(0, tb, drain, 0)
    y = x1_ref[...]
    for k in range(TOP_K):
        y = y + w_ref[:, k:k + 1] * rows_sc[k]
    y_ref[...] = _rms(y, g_ref[...]) if normalize else y


def _combine(x1, dest, top_w, ys, final_g, normalize, tb):
    t, d = x1.shape
    return pl.pallas_call(
        functools.partial(_combine_kernel, normalize=normalize),
        grid=(t // tb,),
        in_specs=[pl.BlockSpec(memory_space=pl.ANY),
                  pl.BlockSpec((tb, d), lambda i: (i, 0)),
                  pl.BlockSpec((tb, TOP_K), lambda i: (i, 0)),
                  pl.BlockSpec((1, d), lambda i: (0, 0)),
                  pl.BlockSpec(memory_space=pl.ANY)],
        out_specs=pl.BlockSpec((tb, d), lambda i: (i, 0)),
        out_shape=jax.ShapeDtypeStruct((t, d), F32),
        scratch_shapes=[pltpu.SMEM((TOP_K * tb,), jnp.int32), pltpu.VMEM((TOP_K, tb, d), F32),
                        pltpu.SemaphoreType.DMA(()), pltpu.SemaphoreType.DMA(())],
        compiler_params=_params(1),
        name="moe_combine",
    )(_block_index_list(dest, tb), x1, top_w.T, final_g.reshape(1, d), ys)


def _block_rows(nact_ref, valid_ref, i):
    nact = nact_ref[0]
    ic = jnp.minimum(i, nact - 1)
    return ic, jnp.where(i < nact, valid_ref[ic], 0)


def _stream_expert_weights(meta, w_hbms, wbuf, w_scs, sems, layer, tn):
    blk_e_ref, nact_ref, valid_ref, first_ref, group_ref, next_ref, ngroups_ref = meta
    n = pl.program_id(0)
    i = pl.program_id(1)
    n_tiles = pl.num_programs(0)
    ic, n_valid = _block_rows(nact_ref, valid_ref, i)

    def copies(expert, slot, tile):
        return [pltpu.make_async_copy(w.at[layer, expert, :, pl.ds(pl.multiple_of(tile * tn, tn), tn)],
                                      wbuf.at[slot, k], sems.at[slot, k]) for k, w in enumerate(w_hbms)]

    @pl.when((n_valid > 0) & (first_ref[ic] == 1))
    def _():
        slot = (group_ref[ic] + n * ngroups_ref[0]) & 1
        mine = copies(blk_e_ref[ic], slot, n)

        @pl.when((i == 0) & (n == 0))
        def _():
            for c in mine:
                c.start()

        for c in mine:
            c.wait()
        nxt = next_ref[ic]

        @pl.when(nxt >= 0)
        def _():
            for c in copies(nxt, 1 - slot, n):
                c.start()

        @pl.when((nxt < 0) & (n + 1 < n_tiles))
        def _():
            for c in copies(blk_e_ref[0], 1 - slot, n + 1):
                c.start()

        for k, w_sc in enumerate(w_scs):
            w_sc[...] = wbuf[slot, k].astype(BF16)

    return n_valid


def _moe_gu_kernel(*refs, sub, layer, tn):
    meta = refs[:7]
    x_ref, wg_hbm, wu_hbm, bg_ref, bu_ref, h_ref, wbuf, wg_sc, wu_sc, sems = refs[7:]
    n_valid = _stream_expert_weights(meta, (wg_hbm, wu_hbm), wbuf, (wg_sc, wu_sc), sems, layer, tn)

    for r0 in range(0, x_ref.shape[0], sub):
        @pl.when(n_valid > r0)
        def _(r0=r0):
            x = _unpack_bf16_pairs(x_ref[r0:r0 + sub, :])
            gt = jnp.minimum(jnp.dot(x, wg_sc[...], preferred_element_type=F32) + bg_ref[...], SWIGLU_LIMIT)
            up = jnp.clip(jnp.dot(x, wu_sc[...], preferred_element_type=F32) + bu_ref[...],
                          -SWIGLU_LIMIT, SWIGLU_LIMIT)
            h_ref[r0:r0 + sub, :] = (gt * jax.nn.sigmoid(SWIGLU_ALPHA * gt) * (up + 1.0)).astype(BF16)

        @pl.when(n_valid <= r0)
        def _(r0=r0):
            h_ref[r0:r0 + sub, :] = jnp.zeros((sub, h_ref.shape[1]), h_ref.dtype)


def _moe_down_kernel(*refs, sub, layer, tn):
    meta = refs[:7]
    h_ref, wd_hbm, bd_ref, y_ref, wbuf, wd_sc, sems = refs[7:]
    n_valid = _stream_expert_weights(meta, (wd_hbm,), wbuf, (wd_sc,), sems, layer, tn)

    for r0 in range(0, h_ref.shape[0], sub):
        @pl.when(n_valid > r0)
        def _(r0=r0):
            y_ref[r0:r0 + sub, :] = (jnp.dot(h_ref[r0:r0 + sub, :], wd_sc[...], preferred_element_type=F32)
                                     + bd_ref[...])

        @pl.when(n_valid <= r0)
        def _(r0=r0):
            y_ref[r0:r0 + sub, :] = jnp.zeros((sub, y_ref.shape[1]), y_ref.dtype)


def _moe_experts(xs, meta, layer, wg, bg, wu, bu, wd, bd, tmb, sub, tn_up, tn_down):
    rows, d_packed = xs.shape
    d = 2 * d_packed
    _, n_exp, _, de = wg.shape
    nblk = rows // tmb
    clamp = lambda i, na: jnp.minimum(i, na[0] - 1)
    lhs = lambda w: pl.BlockSpec((tmb, w), lambda n, i, be, na, *_: (clamp(i, na), 0))
    bspec = lambda tn: pl.BlockSpec((None, 1, tn), lambda n, i, be, na, *_: (be[clamp(i, na)], 0, n))
    out = lambda tn: pl.BlockSpec((tmb, tn), lambda n, i, *_: (i, n))
    hbm = pl.BlockSpec(memory_space=pl.ANY)
    bg, bu, bd = bg[layer], bu[layer], bd[layer]
    hidden = pl.pallas_call(
        functools.partial(_moe_gu_kernel, sub=sub, layer=layer, tn=tn_up),
        grid_spec=pltpu.PrefetchScalarGridSpec(
            num_scalar_prefetch=len(meta), grid=(de // tn_up, nblk),
            in_specs=[lhs(d_packed), hbm, hbm, bspec(tn_up), bspec(tn_up)], out_specs=out(tn_up),
            scratch_shapes=[pltpu.VMEM((2, 2, d, tn_up), F32), pltpu.VMEM((d, tn_up), BF16),
                            pltpu.VMEM((d, tn_up), BF16), pltpu.SemaphoreType.DMA((2, 2))]),
        out_shape=jax.ShapeDtypeStruct((rows, de), BF16),
        compiler_params=_params(2),
        name="moe_gate_up",
    )(*meta, xs, wg, wu, bg.reshape(n_exp, 1, de), bu.reshape(n_exp, 1, de))
    return pl.pallas_call(
        functools.partial(_moe_down_kernel, sub=sub, layer=layer, tn=tn_down),
        grid_spec=pltpu.PrefetchScalarGridSpec(
            num_scalar_prefetch=len(meta), grid=(d // tn_down, nblk),
            in_specs=[lhs(de), hbm, bspec(tn_down)], out_specs=out(tn_down),
            scratch_shapes=[pltpu.VMEM((2, 1, de, tn_down), F32), pltpu.VMEM((de, tn_down), BF16),
                            pltpu.SemaphoreType.DMA((2, 1))]),
        out_shape=jax.ShapeDtypeStruct((rows, d), F32),
        compiler_params=_params(2),
        name="moe_down",
    )(*meta, hidden, wd, bd.reshape(n_exp, 1, d))


def _tile(n, pref):
    return pref if n % pref == 0 else n


def _token_mixers(x, is_prompt, batch, seq, lw, lam_init, sample_state):
    (norm1_g, w_in, w_dw, b_dw, conv_ln_g, conv_ln_b, w_conv_out, b_conv_out, lams, subln_g, w_attn_out, w_o,
     norm2_g) = lw
    t, d = x.shape
    dc = w_dw.shape[1]
    da = N_HEADS * V_DIM
    col_q, col_k, col_v = 2 * dc, 2 * dc + da, 2 * dc + 2 * da
    col_gc, col_ga = 2 * dc + 3 * da, 2 * dc + 3 * da + d
    z = _inproj(x, norm1_g, w_in, _tile(t, 512), 1536)
    k_new = z[:, col_k:col_v].reshape(batch, seq, 2 * N_HEADS, HEAD_DIM)
    v_new = z[:, col_v:col_gc].reshape(batch, seq, N_HEADS, V_DIM)
    if is_prompt:
        u, c_act = _conv_prompt(z, batch, seq, dc, w_dw, b_dw, conv_ln_g, conv_ln_b, _tile(seq, 256))
        conv_new = u.reshape(batch, seq, dc)[:, seq - (CONV_WIDTH - 1):]
        o_norm = _attn_prompt(z, batch, seq, col_q, col_k, col_v, lams, subln_g, lam_init, _tile(seq, 512))
    else:
        state_conv, cache_k, cache_v, layer, page_table, past_len = sample_state
        u, c_act = _conv_sample(z, state_conv, dc, w_dw, b_dw, conv_ln_g, conv_ln_b)
        conv_new = jnp.concatenate([state_conv[:, 1:], u[:, None, :]], axis=1)
        vn_dup = jnp.repeat(v_new.reshape(t, N_HEADS, V_DIM), 2, axis=1)
        cache_kt = jnp.transpose(cache_k, (0, 1, 3, 4, 2))
        o_maps = _attn_decode(z[:, col_q:col_k].reshape(t, 1, da), z[:, col_k:col_v].reshape(t, 1, da), vn_dup,
                              cache_kt, cache_v, layer, page_table, past_len)
        o_norm = _diff_norm(o_maps.reshape(t * N_HEADS, 2 * V_DIM), lams, subln_g, lam_init).reshape(t, da)
    x1, h2 = _merge(x, c_act, o_norm, z, col_gc, col_ga, w_conv_out, b_conv_out, w_attn_out, w_o, norm2_g,
                    _tile(t, 256))
    return x1, h2, k_new, v_new, conv_new


def _moe(groups, w_router, b_router, layer, wg, bg, wu, bu, wd, bd, final_g, normalize):
    _, n_exp, d, _ = wg.shape
    tmb, sub = MOE_ROW_BLOCK, MOE_ROW_SUBBLOCK
    wr_t = w_router.T.astype(BF16)
    base = jnp.zeros((n_exp, 1), F32)
    routed = []
    for x1, h2 in groups:
        top_i, top_w, rank, base = _router(h2, wr_t, b_router, base, _tile(h2.shape[0], 512))
        routed.append((top_i, top_w, rank))
    counts = base[:, 0].astype(jnp.int32)
    padded = (counts + tmb - 1) // tmb * tmb
    pend = jnp.cumsum(padded)
    pstart = pend - padded
    n_assign = sum(h2.shape[0] for _, h2 in groups) * TOP_K
    nblk = pl.cdiv(n_assign, tmb) + n_exp
    nact = (pend[-1] // tmb).astype(jnp.int32).reshape(1)
    blk_row = jnp.arange(nblk, dtype=jnp.int32) * tmb
    blk_e = jnp.minimum(jnp.sum(pend[None, :] <= blk_row[:, None], axis=1), n_exp - 1).astype(jnp.int32)
    experts = jnp.arange(n_exp, dtype=jnp.int32)
    blk_last = jnp.sum(jnp.where(blk_e[:, None] == experts[None, :], (pstart + counts)[None, :], 0), axis=1)
    blk_valid = jnp.clip(blk_last - blk_row, 0, tmb).astype(jnp.int32)
    blk_id = jnp.arange(nblk, dtype=jnp.int32)
    active = blk_id < nact[0]
    blk_first = (active & (blk_e != jnp.concatenate([jnp.full((1,), -1, jnp.int32), blk_e[:-1]]))).astype(jnp.int32)
    blk_group = jnp.cumsum(blk_first) - 1
    later_other = active[None, :] & (blk_id[None, :] > blk_id[:, None]) & (blk_e[None, :] != blk_e[:, None])
    nxt_blk = jnp.min(jnp.where(later_other, blk_id[None, :], nblk), axis=1)
    blk_next = jnp.sum(jnp.where(blk_id[None, :] == nxt_blk[:, None], blk_e[None, :] + 1, 0), axis=1) - 1
    meta = (blk_e, nact, blk_valid, blk_first, blk_group.astype(jnp.int32), blk_next.astype(jnp.int32),
            jnp.sum(blk_first).astype(jnp.int32).reshape(1))
    xs = jnp.zeros((nblk * tmb, d // 2), jnp.uint32)
    dests = []
    for (x1, h2), (top_i, top_w, rank) in zip(groups, routed):
        first = jnp.sum(jnp.where(top_i[None] == experts[:, None, None], pstart[:, None, None], 0), axis=0)
        dest = first + rank
        dests.append(dest)
        xs = _dispatch(h2, dest, xs, _tile(h2.shape[0], 256))
    ys = _moe_experts(xs, meta, layer, wg, bg, wu, bu, wd, bd, tmb, sub, MOE_UP_COLS, MOE_DOWN_COLS)
    outs = []
    for (x1, h2), (top_i, top_w, rank), dest in zip(groups, routed, dests):
        outs.append(_combine(x1, dest, top_w, ys, final_g, normalize, _tile(x1.shape[0], 256)))
    return outs


def kernel(x_prompt, x_sample, cache_k, cache_v, state_conv, page_table, norm1_g, w_in, w_dw, b_dw, conv_ln_g, conv_ln_b, w_conv_out, b_conv_out, lambda_q1, lambda_k1, lambda_q2, lambda_k2, subln_g, w_attn_out, w_o, norm2_g, w_router, b_router, w_exp_gate, b_exp_gate, w_exp_up, b_exp_up, w_exp_down, b_exp_down, final_norm_g):
    depth = w_in.shape[0]
    batch, seq, d = x_prompt.shape
    dec_b, dec_seq, _ = x_sample.shape
    assert dec_seq == 1
    past_len = page_table.shape[1] * cache_k.shape[2]
    xp = x_prompt.reshape(batch * seq, d)
    xs = x_sample.reshape(dec_b * dec_seq, d)
    outs = [[] for _ in range(6)]
    for l in range(depth):
        lam_init = 0.8 - 0.6 * math.exp(-0.3 * l)
        lams = (lambda_q1[l], lambda_k1[l], lambda_q2[l], lambda_k2[l])
        lw = (norm1_g[l], w_in[l].astype(BF16), w_dw[l], b_dw[l], conv_ln_g[l], conv_ln_b[l],
              w_conv_out[l].astype(BF16), b_conv_out[l], lams, subln_g[l], w_attn_out[l].astype(BF16),
              w_o[l].astype(BF16), norm2_g[l])
        x1p, h2p, kp, vp, cp = _token_mixers(xp, True, batch, seq, lw, lam_init, None)
        x1s, h2s, ks, vs, cs = _token_mixers(xs, False, dec_b, dec_seq, lw, lam_init,
                                             (state_conv[l], cache_k, cache_v, l, page_table, past_len))
        xp, xs = _moe([(x1p, h2p), (x1s, h2s)], w_router[l], b_router[l], l, w_exp_gate, b_exp_gate,
                      w_exp_up, b_exp_up, w_exp_down, b_exp_down, final_norm_g, l == depth - 1)
        for lst, v in zip(outs, (kp, vp, cp, ks, vs, cs)):
            lst.append(v)
    y_prompt = xp.reshape(batch, seq, d)
    y_sample = xs.reshape(dec_b, dec_seq, d)
    return (y_prompt, y_sample) + tuple(jnp.stack(v) for v in outs)
```

```python
import functools
import math

import jax
import jax.numpy as jnp
from jax import lax
from jax.experimental import pallas as pl
from jax.experimental.pallas import tpu as pltpu

RMS_EPS = 1e-5
LN_EPS = 1e-5
CONV_WIDTH = 31
CONV_HALO = 32
N_HEADS = 8
HEAD_DIM = 64
V_DIM = 128
HEAD_SHIFT = 6
TOP_K = 4
SWIGLU_ALPHA = 1.702
SWIGLU_LIMIT = 7.0
DECODE_PAGES_PER_STEP = 8
MOE_ROW_BLOCK = 512
MOE_ROW_SUBBLOCK = 256
MOE_UP_COLS = 512
MOE_DOWN_COLS = 1024
NEG_BIG = -1e30

V7X_VMEM_LIMIT_BYTES = 56 * 1024 * 1024
V7X_LANES = 128
V7X_SUBLANES = 8
BF16 = jnp.bfloat16
F32 = jnp.float32


def _params(n_axes):
    return pltpu.CompilerParams(dimension_semantics=("arbitrary",) * n_axes,
                                vmem_limit_bytes=V7X_VMEM_LIMIT_BYTES)


def _rms(x, g):
    return x * lax.rsqrt(jnp.mean(x * x, axis=-1, keepdims=True) + RMS_EPS) * g


def _lam_value(lq1, lk1, lq2, lk2, lam_init):
    s1 = jnp.sum(lq1 * lk1, axis=-1, keepdims=True)
    s2 = jnp.sum(lq2 * lk2, axis=-1, keepdims=True)
    return jnp.exp(s1) - jnp.exp(s2) + lam_init


def _inproj_kernel(x_ref, g_ref, w_ref, z_ref, *rest, kt_tile):
    @pl.when(pl.program_id(1) == 0)
    def _():
        rest[-1][...] = _rms(x_ref[...], g_ref[...]).astype(BF16)

    z = jnp.dot(rest[-1][...], w_ref[...], preferred_element_type=F32)
    z_ref[...] = z
    if kt_tile is not None:
        @pl.when(pl.program_id(1) == kt_tile)
        def _():
            rest[0][...] = z.T


def _inproj(x, g, w_bf16, tm, tn, kt=None):
    t, d = x.shape
    n = w_bf16.shape[1]
    out_specs = [pl.BlockSpec((tm, tn), lambda i, j: (i, j))]
    out_shape = [jax.ShapeDtypeStruct((t, n), F32)]
    kt_tile = None
    if kt is not None:
        batch, seq, col_k = kt
        per_seq = seq // tm
        kt_tile = col_k // tn
        out_specs.append(pl.BlockSpec((None, tn, tm), lambda i, j: (i // per_seq, 0, i % per_seq)))
        out_shape.append(jax.ShapeDtypeStruct((batch, tn, seq), F32))
    return pl.pallas_call(
        functools.partial(_inproj_kernel, kt_tile=kt_tile),
        grid=(t // tm, n // tn),
        in_specs=[pl.BlockSpec((tm, d), lambda i, j: (i, 0)),
                  pl.BlockSpec((1, d), lambda i, j: (0, 0)),
                  pl.BlockSpec((d, tn), lambda i, j: (0, j))],
        out_specs=out_specs,
        out_shape=out_shape,
        scratch_shapes=[pltpu.VMEM((tm, d), BF16)],
        compiler_params=_params(2),
        name="inproj",
    )(x, g.reshape(1, d), w_bf16)


def _ln_silu(y, lng, lnb):
    mu = jnp.mean(y, axis=-1, keepdims=True)
    yc = y - mu
    var = jnp.mean(yc * yc, axis=-1, keepdims=True)
    yn = yc * lax.rsqrt(var + LN_EPS) * lng + lnb
    return yn * jax.nn.sigmoid(yn)


def _conv_prompt_kernel(a_ref, g_ref, wdw_ref, bdw_ref, lng_ref, lnb_ref, u_ref, c_ref, win_sc, shift_sc, *, rows):
    ts = a_ref.shape[0]
    sub = shift_sc.shape[0]

    @pl.when(pl.program_id(1) == 0)
    def _():
        win_sc[0:CONV_HALO, :] = jnp.zeros((CONV_HALO, win_sc.shape[1]), F32)

    u = a_ref[...] * jax.nn.sigmoid(g_ref[...])
    u_ref[...] = u
    win_sc[CONV_HALO:CONV_HALO + ts, :] = u
    span = shift_sc.shape[1]
    for r in range(1, sub):
        shift_sc[r] = win_sc[r:r + span, :]
    off = CONV_HALO - (CONV_WIDTH - 1)
    for r0 in range(0, ts, rows):
        acc = jnp.zeros((rows, u.shape[1]), F32)
        for j in range(CONV_WIDTH):
            whole, r = divmod(off + j, sub)
            lo = r0 + whole * sub
            tap = win_sc[lo:lo + rows, :] if r == 0 else shift_sc[r, lo:lo + rows, :]
            acc = acc + wdw_ref[j:j + 1, :] * tap
        y = acc + bdw_ref[...]
        c_ref[r0:r0 + rows, :] = _ln_silu(y, lng_ref[...], lnb_ref[...]).astype(BF16)
    win_sc[0:CONV_HALO, :] = win_sc[ts:ts + CONV_HALO, :]


def _conv_prompt(z, batch, seq, dc, w_dw, b_dw, ln_g, ln_b, ts):
    t = z.shape[0]
    nt = seq // ts
    vec = lambda v: v.reshape(1, dc)
    const = lambda shape: pl.BlockSpec(shape, lambda b, i: (0, 0))
    return pl.pallas_call(
        functools.partial(_conv_prompt_kernel, rows=32),
        grid=(batch, nt),
        in_specs=[pl.BlockSpec((ts, dc), lambda b, i: (b * nt + i, 0)),
                  pl.BlockSpec((ts, dc), lambda b, i: (b * nt + i, 1)),
                  const((CONV_WIDTH, dc)), const((1, dc)), const((1, dc)), const((1, dc))],
        out_specs=[pl.BlockSpec((ts, dc), lambda b, i: (b * nt + i, 0)),
                   pl.BlockSpec((ts, dc), lambda b, i: (b * nt + i, 0))],
        out_shape=[jax.ShapeDtypeStruct((t, dc), F32), jax.ShapeDtypeStruct((t, dc), BF16)],
        scratch_shapes=[pltpu.VMEM((ts + CONV_HALO, dc), F32),
                        pltpu.VMEM((V7X_SUBLANES, ts + CONV_HALO - V7X_SUBLANES, dc), F32)],
        compiler_params=_params(2),
        name="conv_prompt",
    )(z, z, w_dw, vec(b_dw), vec(ln_g), vec(ln_b))


def _conv_sample_kernel(a_ref, g_ref, st_ref, wdw_ref, bdw_ref, lng_ref, lnb_ref, u_ref, c_ref):
    nb = a_ref.shape[0]
    hist = CONV_WIDTH - 1
    u = a_ref[...] * jax.nn.sigmoid(g_ref[...])
    u_ref[...] = u

    def body(b, carry):
        past = jnp.sum(st_ref[b] * wdw_ref[0:hist, :], axis=0, keepdims=True)
        ub = u_ref[pl.ds(b, 1), :]
        y = past + ub * wdw_ref[hist:hist + 1, :] + bdw_ref[...]
        c_ref[pl.ds(b, 1), :] = _ln_silu(y, lng_ref[...], lnb_ref[...])
        return carry

    lax.fori_loop(0, nb, body, 0)


def _conv_sample(z, state, dc, w_dw, b_dw, ln_g, ln_b):
    nb = z.shape[0]
    vec = lambda v: v.reshape(1, dc)
    const = lambda shape: pl.BlockSpec(shape, lambda i: (0,) * len(shape))
    return pl.pallas_call(
        _conv_sample_kernel,
        grid=(1,),
        in_specs=[pl.BlockSpec((nb, dc), lambda i: (0, 0)),
                  pl.BlockSpec((nb, dc), lambda i: (0, 1)),
                  const(state.shape), const((CONV_WIDTH, dc)), const((1, dc)), const((1, dc)), const((1, dc))],
        out_specs=[const((nb, dc)), const((nb, dc))],
        out_shape=[jax.ShapeDtypeStruct((nb, dc), F32), jax.ShapeDtypeStruct((nb, dc), F32)],
        compiler_params=_params(1),
        name="conv_sample",
    )(z, z, state, w_dw, vec(b_dw), vec(ln_g), vec(ln_b))


def _attn_prompt_kernel(slopes_ref, q_ref, k_ref, v_ref, lq1_ref, lk1_ref, lq2_ref, lk2_ref, g_ref, o_ref,
                        vt_sc, kb_sc, acc_sc, *, lam_init):
    tq = q_ref.shape[0]
    n_kv = vt_sc.shape[0]
    h = pl.program_id(1)
    qi = pl.program_id(2)
    slope = slopes_ref[h]
    lam = _lam_value(lq1_ref[...], lk1_ref[...], lq2_ref[...], lk2_ref[...], lam_init)

    @pl.when(qi == 0)
    def _():
        for jb in range(n_kv):
            vt_sc[jb] = v_ref[jb * tq:(jb + 1) * tq, :].T.astype(BF16)
        kb_sc[...] = slope * lax.broadcasted_iota(jnp.int32, kb_sc.shape, 0).astype(F32)

    q = (q_ref[...] * (HEAD_DIM ** -0.5)).astype(BF16)
    lane = lax.broadcasted_iota(jnp.int32, q.shape, 1)
    zero = jnp.zeros_like(q)
    qq = jnp.concatenate([jnp.where(lane < HEAD_DIM, q, zero), jnp.where(lane >= HEAD_DIM, q, zero)], axis=0)
    acc_sc[...] = jnp.zeros(acc_sc.shape, F32)

    def step(j, carry, diagonal):
        m, l = carry
        kb = k_ref[pl.ds(pl.multiple_of(j * tq, tq), tq), :].astype(BF16)
        s = lax.dot_general(kb, qq, (((1,), (1,)), ((), ())), preferred_element_type=F32)
        bias = kb_sc[...]
        s = s + jnp.concatenate([bias] * (2 * tq // bias.shape[1]), axis=1)
        if diagonal:
            r = lax.broadcasted_iota(jnp.int32, s.shape, 0)
            c = lax.broadcasted_iota(jnp.int32, s.shape, 1)
            c = jnp.where(c >= tq, c - tq, c)
            s = jnp.where(r <= c, s, -jnp.inf)
        off = slope * ((j - qi) * tq).astype(F32)
        m_new = jnp.maximum(m, jnp.max(s, axis=0, keepdims=True) + off)
        alpha = jnp.exp(m - m_new)
        p = jnp.exp(s - (m_new - off))
        l = alpha * l + jnp.sum(p, axis=0, keepdims=True)
        acc_sc[...] = alpha * acc_sc[...] + jnp.dot(vt_sc[j], p.astype(BF16), preferred_element_type=F32)
        return m_new, l

    init = (jnp.full((1, 2 * tq), NEG_BIG, F32), jnp.zeros((1, 2 * tq), F32))
    carry = lax.fori_loop(0, qi, lambda j, c: step(j, c, False), init)
    _, l = step(qi, carry, True)
    o = acc_sc[...] / l
    d = o[:, :tq] - lam * o[:, tq:]
    dn = d * lax.rsqrt(jnp.mean(d * d, axis=0, keepdims=True) + RMS_EPS)
    o_ref[...] = (dn.T * g_ref[...] * (1.0 - lam_init)).astype(BF16)


def _attn_prompt(z, batch, seq, col_q, col_k, col_v, lams, subln_g, lam_init, tq):
    t = z.shape[0]
    nq = seq // tq
    slopes = 2.0 ** (-8.0 * jnp.arange(1, N_HEADS + 1, dtype=F32) / N_HEADS)
    blk = 2 * HEAD_DIM
    lam_spec = pl.BlockSpec((1, HEAD_DIM), lambda b, h, i, s: (0, 0))
    grid_spec = pltpu.PrefetchScalarGridSpec(
        num_scalar_prefetch=1,
        grid=(batch, N_HEADS, nq),
        in_specs=[pl.BlockSpec((tq, blk), lambda b, h, i, s: (b * nq + i, col_q // blk + h)),
                  pl.BlockSpec((seq, blk), lambda b, h, i, s: (b, col_k // blk + h)),
                  pl.BlockSpec((seq, V_DIM), lambda b, h, i, s: (b, col_v // V_DIM + h)),
                  lam_spec, lam_spec, lam_spec, lam_spec,
                  pl.BlockSpec((1, V_DIM), lambda b, h, i, s: (0, 0))],
        out_specs=pl.BlockSpec((tq, V_DIM), lambda b, h, i, s: (b * nq + i, h)),
        scratch_shapes=[pltpu.VMEM((nq, V_DIM, tq), BF16), pltpu.VMEM((tq, V7X_LANES), F32),
                        pltpu.VMEM((V_DIM, 2 * tq), F32)],
    )
    return pl.pallas_call(
        functools.partial(_attn_prompt_kernel, lam_init=lam_init),
        grid_spec=grid_spec,
        out_shape=jax.ShapeDtypeStruct((t, N_HEADS * V_DIM), BF16),
        compiler_params=_params(3),
        name="attn_prompt",
    )(slopes, z, z, z, *[v.reshape(1, HEAD_DIM) for v in lams], subln_g.reshape(1, V_DIM))


def _attn_decode_kernel(pt_ref, q_ref, kn_ref, vn_ref, spread_ref, *refs, past_len, pages):
    kc_refs, vc_refs = refs[:pages], refs[pages:2 * pages]
    o_ref, m_sc, l_sc, acc_sc, v_sc = refs[2 * pages:]
    p = pl.program_id(1)
    n_steps = pl.num_programs(1)
    n_maps = 2 * N_HEADS
    keys = kc_refs[0].shape[2]

    @pl.when(p == 0)
    def _():
        m_sc[...] = jnp.full(m_sc.shape, NEG_BIG, F32)
        l_sc[...] = jnp.zeros(l_sc.shape, F32)
        acc_sc[...] = jnp.zeros(acc_sc.shape, F32)

    q_row = q_ref[...] * (HEAD_DIM ** -0.5)
    qrow_i = lax.broadcasted_iota(jnp.int32, (n_maps, n_maps * HEAD_DIM), 0)
    qcol_i = lax.broadcasted_iota(jnp.int32, (n_maps, n_maps * HEAD_DIM), 1)
    qbd = jnp.where((qcol_i >> HEAD_SHIFT) == qrow_i, q_row, 0.0).astype(BF16)
    row1 = lax.broadcasted_iota(jnp.int32, (n_maps, 1), 0)
    slope = jnp.exp2(-((row1 >> 1) + 1).astype(F32))

    s = jnp.concatenate(
        [jnp.dot(qbd, kc_ref[...].reshape(n_maps * HEAD_DIM, keys).astype(BF16), preferred_element_type=F32)
         for kc_ref in kc_refs], axis=1)
    k_pos = p * (pages * keys) + lax.broadcasted_iota(jnp.int32, s.shape, 1)
    s = s - slope * (past_len - k_pos).astype(F32)

    m_prev = m_sc[...]
    m_new = jnp.maximum(m_prev, jnp.max(s, axis=1, keepdims=True))
    alpha = jnp.exp(m_prev - m_new)
    pe = jnp.exp(s - m_new)
    l_sc[...] = alpha * l_sc[...] + jnp.sum(pe, axis=1, keepdims=True)
    m_sc[...] = m_new

    pb = pe.astype(BF16)
    stacked = jnp.concatenate([pb[:, i * keys:(i + 1) * keys] for i in range(pages)], axis=0)
    spread = jnp.dot(stacked, spread_ref[...], preferred_element_type=F32)
    srow = lax.broadcasted_iota(jnp.int32, spread.shape, 0)
    scol = lax.broadcasted_iota(jnp.int32, spread.shape, 1)
    own_head = (scol & (N_HEADS - 1)) == ((srow & (n_maps - 1)) >> 1)
    pc = jnp.where(own_head, spread, 0.0).astype(BF16)
    pc = jnp.concatenate([pc[i * n_maps:(i + 1) * n_maps, :] for i in range(pages)], axis=1)
    rows = keys * N_HEADS
    for i, vc_ref in enumerate(vc_refs):
        v_sc[i * rows:(i + 1) * rows, :] = vc_ref[...].reshape(rows, V_DIM).astype(BF16)
    acc_sc[...] = alpha * acc_sc[...] + jnp.dot(pc, v_sc[...], preferred_element_type=F32)

    @pl.when(p == n_steps - 1)
    def _():
        kn = kn_ref[...].astype(BF16).astype(F32)
        s_new = jnp.sum(qbd.astype(F32) * kn, axis=1, keepdims=True)
        m_old = m_sc[...]
        m_fin = jnp.maximum(m_old, s_new)
        a = jnp.exp(m_old - m_fin)
        p_new = jnp.exp(s_new - m_fin)
        l_fin = a * l_sc[...] + p_new
        vn = vn_ref[...].astype(BF16).astype(F32)
        acc = a * acc_sc[...] + p_new.astype(BF16).astype(F32) * vn
        o_ref[...] = acc / l_fin


def _attn_decode(q_s, kn_s, vn_dup, cache_kt, cache_v, layer, page_table, past_len):
    nb, n_pages = page_table.shape
    n_maps = 2 * N_HEADS
    page = cache_kt.shape[-1]
    pages = DECODE_PAGES_PER_STEP if n_pages % DECODE_PAGES_PER_STEP == 0 else 1
    spread = (jnp.arange(page)[:, None] == jnp.arange(page * N_HEADS)[None, :] // N_HEADS).astype(BF16)

    def page_spec(shape, i):
        return pl.BlockSpec((None, None) + shape,
                            lambda b, p, pt: (layer, pt[b * n_pages + p * pages + i], 0, 0, 0))

    grid_spec = pltpu.PrefetchScalarGridSpec(
        num_scalar_prefetch=1,
        grid=(nb, n_pages // pages),
        in_specs=[pl.BlockSpec((None, 1, n_maps * HEAD_DIM), lambda b, p, pt: (b, 0, 0)),
                  pl.BlockSpec((None, 1, n_maps * HEAD_DIM), lambda b, p, pt: (b, 0, 0)),
                  pl.BlockSpec((None, n_maps, V_DIM), lambda b, p, pt: (b, 0, 0)),
                  pl.BlockSpec((page, page * N_HEADS), lambda b, p, pt: (0, 0))]
                 + [page_spec((n_maps, HEAD_DIM, page), i) for i in range(pages)]
                 + [page_spec((page, N_HEADS, V_DIM), i) for i in range(pages)],
        out_specs=pl.BlockSpec((None, n_maps, V_DIM), lambda b, p, pt: (b, 0, 0)),
        scratch_shapes=[pltpu.VMEM((n_maps, 1), F32), pltpu.VMEM((n_maps, 1), F32),
                        pltpu.VMEM((n_maps, V_DIM), F32), pltpu.VMEM((pages * page * N_HEADS, V_DIM), BF16)],
    )
    return pl.pallas_call(
        functools.partial(_attn_decode_kernel, past_len=past_len, pages=pages),
        grid_spec=grid_spec,
        out_shape=jax.ShapeDtypeStruct((nb, n_maps, V_DIM), F32),
        compiler_params=_params(2),
        name="attn_decode",
    )(page_table.reshape(-1), q_s, kn_s, vn_dup, spread, *([cache_kt] * pages), *([cache_v] * pages))


def _diff_norm_kernel(o_ref, lq1_ref, lk1_ref, lq2_ref, lk2_ref, g_ref, out_ref, *, lam_init):
    lam = _lam_value(lq1_ref[...], lk1_ref[...], lq2_ref[...], lk2_ref[...], lam_init)
    o = o_ref[...]
    d = o[:, :V_DIM] - lam * o[:, V_DIM:]
    out_ref[...] = _rms(d, g_ref[...]) * (1.0 - lam_init)


def _diff_norm(o_pairs, lams, subln_g, lam_init):
    rows = o_pairs.shape[0]
    const = lambda shape: pl.BlockSpec(shape, lambda i: (0, 0))
    return pl.pallas_call(
        functools.partial(_diff_norm_kernel, lam_init=lam_init),
        grid=(1,),
        in_specs=[const(o_pairs.shape)] + [const((1, HEAD_DIM))] * 4 + [const((1, V_DIM))],
        out_specs=const((rows, V_DIM)),
        out_shape=jax.ShapeDtypeStruct((rows, V_DIM), F32),
        compiler_params=_params(1),
        name="diff_norm",
    )(o_pairs, *[v.reshape(1, HEAD_DIM) for v in lams], subln_g.reshape(1, V_DIM))


def _pack_bf16_pairs(x):
    n = x.shape[1] // 2
    bits = lax.bitcast_convert_type(x.astype(BF16).astype(F32), jnp.uint32)
    return (bits[:, :n] >> 16) | (bits[:, n:] & jnp.uint32(0xFFFF0000))


def _unpack_bf16_pairs(w):
    lo = lax.bitcast_convert_type(w << 16, F32)
    hi = lax.bitcast_convert_type(w & jnp.uint32(0xFFFF0000), F32)
    return jnp.concatenate([lo, hi], axis=1).astype(BF16)


def _merge_kernel(x_ref, c_ref, o_ref, gc0_ref, gc1_ref, ga0_ref, ga1_ref, wco_ref, bco_ref, wao_ref, wo_ref,
                  g2_ref, x1_ref, h2_ref):
    half = gc0_ref.shape[1]
    conv_out = jnp.dot(c_ref[...].astype(BF16), wco_ref[...], preferred_element_type=F32) + bco_ref[...]
    attn_out = jnp.dot(o_ref[...].astype(BF16), wao_ref[...], preferred_element_type=F32)
    mixed = []
    for n, (gc_ref, ga_ref) in enumerate(((gc0_ref, ga0_ref), (gc1_ref, ga1_ref))):
        sl = slice(n * half, (n + 1) * half)
        mixed.append((jax.nn.sigmoid(gc_ref[...]) * conv_out[:, sl]
                      + jax.nn.sigmoid(ga_ref[...]) * attn_out[:, sl]).astype(BF16))
    x1 = x_ref[...] + jnp.dot(jnp.concatenate(mixed, axis=1), wo_ref[...], preferred_element_type=F32)
    x1_ref[...] = x1
    h2_ref[...] = _pack_bf16_pairs(_rms(x1, g2_ref[...]))


def _merge(x, c_act, o_norm, z, col_gc, col_ga, wco, bco, wao, wo, g2, tm):
    t, d = x.shape
    dc, da = c_act.shape[1], o_norm.shape[1]
    half = d // 2
    row = lambda w: pl.BlockSpec((tm, w), lambda i: (i, 0))
    gate = lambda blk: pl.BlockSpec((tm, half), lambda i: (i, blk))
    weight = lambda shape: pl.BlockSpec(shape, lambda i: (0, 0), pipeline_mode=pl.Buffered(1))
    return pl.pallas_call(
        _merge_kernel,
        grid=(t // tm,),
        in_specs=[row(d), row(dc), row(da),
                  gate(col_gc // half), gate(col_gc // half + 1), gate(col_ga // half), gate(col_ga // half + 1),
                  weight((dc, d)), weight((1, d)), weight((da, d)), weight((d, d)), weight((1, d))],
        out_specs=[row(d), row(half)],
        out_shape=[jax.ShapeDtypeStruct((t, d), F32), jax.ShapeDtypeStruct((t, half), jnp.uint32)],
        compiler_params=_params(1),
        name="merge",
    )(x, c_act, o_norm, z, z, z, z, wco, bco.reshape(1, d), wao, wo, g2.reshape(1, d))


def _router_kernel(h_ref, wr_ref, br_ref, base_ref, tri_ref, ti_ref, tw_ref, rk_ref, cnt_ref, cnt_sc, *, n_tokens):
    i = pl.program_id(0)
    tb = h_ref.shape[0]
    n_exp = wr_ref.shape[0]

    @pl.when(i == 0)
    def _():
        cnt_sc[...] = base_ref[...]

    logits = lax.dot_general(wr_ref[...], _unpack_bf16_pairs(h_ref[...]), (((1,), (1,)), ((), ())),
                             preferred_element_type=F32) + br_ref[...]
    tok = i * tb + lax.broadcasted_iota(jnp.int32, (1, tb), 1)
    valid = tok < n_tokens
    eidx = lax.broadcasted_iota(jnp.int32, (n_exp, tb), 0)
    run = cnt_sc[...]
    x = logits
    vals = []
    for k in range(TOP_K):
        mx = jnp.max(x, axis=0, keepdims=True)
        sel = jnp.min(jnp.where(x == mx, eidx, n_exp), axis=0, keepdims=True)
        hit = eidx == sel
        x = jnp.where(hit, -jnp.inf, x)
        onehot = jnp.where(hit & valid, 1.0, 0.0)
        before = jnp.dot(onehot.astype(BF16), tri_ref[...], preferred_element_type=F32)
        rank = jnp.sum(onehot * (before + run), axis=0, keepdims=True)
        run = run + jnp.sum(onehot, axis=1, keepdims=True)
        vals.append(mx)
        ti_ref[k:k + 1, :] = sel
        rk_ref[k:k + 1, :] = rank.astype(jnp.int32)
    es = [jnp.exp(v - vals[0]) for v in vals]
    tot = es[0] + es[1] + es[2] + es[3]
    for k in range(TOP_K):
        tw_ref[k:k + 1, :] = es[k] / tot
    cnt_sc[...] = run
    cnt_ref[...] = run


def _router(h2, wr_t, b_router, base, tb):
    t, d = h2.shape
    n_exp = wr_t.shape[0]
    nb = pl.cdiv(t, tb)
    tri = (jnp.arange(tb)[:, None] < jnp.arange(tb)[None, :]).astype(BF16)
    const = lambda shape: pl.BlockSpec(shape, lambda i: (0, 0))
    tok = lambda dt: jax.ShapeDtypeStruct((TOP_K, t), dt)
    return pl.pallas_call(
        functools.partial(_router_kernel, n_tokens=t),
        grid=(nb,),
        in_specs=[pl.BlockSpec((tb, d), lambda i: (i, 0)), const(wr_t.shape), const((n_exp, 1)),
                  const((n_exp, 1)), const((tb, tb))],
        out_specs=[pl.BlockSpec((TOP_K, tb), lambda i: (0, i))] * 3 + [const((n_exp, 1))],
        out_shape=[tok(jnp.int32), tok(F32), tok(jnp.int32), jax.ShapeDtypeStruct((n_exp, 1), F32)],
        scratch_shapes=[pltpu.VMEM((n_exp, 1), F32)],
        compiler_params=_params(1),
        name="router",
    )(h2, wr_t, b_router.reshape(n_exp, 1), base, tri)


def _block_index_list(dest, tb):
    t = dest.shape[1]
    return dest.reshape(TOP_K, t // tb, tb).transpose(1, 0, 2).reshape(-1)


def _dispatch_kernel(dest_ref, h_ref, xs_in_ref, xs_ref, idx_sm, idx_sem, row_sem):
    del xs_in_ref
    i = pl.program_id(0)
    tb = h_ref.shape[0]
    n_idx = TOP_K * tb
    cp = pltpu.make_async_copy(dest_ref.at[pl.ds(pl.multiple_of(i * n_idx, n_idx), n_idx)], idx_sm, idx_sem)
    cp.start()
    cp.wait()

    def row_copy(t, k):
        d = idx_sm[k * tb + t]
        return pltpu.make_async_copy(h_ref.at[pl.ds(t, 1), :], xs_ref.at[pl.ds(d, 1), :], row_sem)

    _run_row_copies(tb, row_copy)


def _run_row_copies(tb, row_copy):
    def issue(t, carry):
        for k in range(TOP_K):
            row_copy(t, k).start()
        return carry

    def drain(t, carry):
        for k in range(TOP_K):
            row_copy(t, k).wait()
        return carry

    lax.fori_loop(0, tb, issue, 0)
    lax.fori_loop(0, tb, drain, 0)


def _dispatch(h2, dest, xs, tb):
    t, d = h2.shape
    return pl.pallas_call(
        _dispatch_kernel,
        grid=(t // tb,),
        in_specs=[pl.BlockSpec(memory_space=pl.ANY),
                  pl.BlockSpec((tb, d), lambda i: (i, 0)),
                  pl.BlockSpec(memory_space=pl.ANY)],
        out_specs=pl.BlockSpec(memory_space=pl.ANY),
        out_shape=jax.ShapeDtypeStruct(xs.shape, xs.dtype),
        scratch_shapes=[pltpu.SMEM((TOP_K * tb,), jnp.int32), pltpu.SemaphoreType.DMA(()),
                        pltpu.SemaphoreType.DMA(())],
        input_output_aliases={2: 0},
        compiler_params=_params(1),
        name="moe_dispatch",
    )(_block_index_list(dest, tb), h2, xs)


def _combine_kernel(dest_ref, x1_ref, w_ref, g_ref, ys_ref, y_ref, idx_sm, rows_sc, idx_sem, row_sem, *, normalize):
    i = pl.program_id(0)
    tb = x1_ref.shape[0]
    n_idx = TOP_K * tb
    cp = pltpu.make_async_copy(dest_ref.at[pl.ds(pl.multiple_of(i * n_idx, n_idx), n_idx)], idx_sm, idx_sem)
    cp.start()
    cp.wait()

    def row_copy(t, k):
        d = idx_sm[k * tb + t]
        return pltpu.make_async_copy(ys_ref.at[pl.ds(d, 1), :], rows_sc.at[k, pl.ds(t, 1), :], row_sem)

    _run_row_copies(tb, row_copy)
    y = x1_ref[...]
    for k in range(TOP_K):
        y = y + w_ref[:, k:k + 1] * rows_sc[k]
    y_ref[...] = _rms(y, g_ref[...]) if normalize else y


def _combine(x1, dest, top_w, ys, final_g, normalize, tb):
    t, d = x1.shape
    return pl.pallas_call(
        functools.partial(_combine_kernel, normalize=normalize),
        grid=(t // tb,),
        in_specs=[pl.BlockSpec(memory_space=pl.ANY),
                  pl.BlockSpec((tb, d), lambda i: (i, 0)),
                  pl.BlockSpec((tb, TOP_K), lambda i: (i, 0)),
                  pl.BlockSpec((1, d), lambda i: (0, 0)),
                  pl.BlockSpec(memory_space=pl.ANY)],
        out_specs=pl.BlockSpec((tb, d), lambda i: (i, 0)),
        out_shape=jax.ShapeDtypeStruct((t, d), F32),
        scratch_shapes=[pltpu.SMEM((TOP_K * tb,), jnp.int32), pltpu.VMEM((TOP_K, tb, d), F32),
                        pltpu.SemaphoreType.DMA(()), pltpu.SemaphoreType.DMA(())],
        compiler_params=_params(1),
        name="moe_combine",
    )(_block_index_list(dest, tb), x1, top_w.T, final_g.reshape(1, d), ys)


def _block_rows(nact_ref, valid_ref, i):
    nact = nact_ref[0]
    ic = jnp.minimum(i, nact - 1)
    return ic, jnp.where(i < nact, valid_ref[ic], 0)


def _stream_expert_weights(meta, w_hbms, wbuf, w_scs, sems, layer, tn):
    blk_e_ref, nact_ref, valid_ref, first_ref, group_ref, next_ref, ngroups_ref = meta
    n = pl.program_id(0)
    i = pl.program_id(1)
    n_tiles = pl.num_programs(0)
    ic, n_valid = _block_rows(nact_ref, valid_ref, i)

    def copies(expert, slot, tile):
        return [pltpu.make_async_copy(w.at[layer, expert, :, pl.ds(pl.multiple_of(tile * tn, tn), tn)],
                                      wbuf.at[slot, k], sems.at[slot, k]) for k, w in enumerate(w_hbms)]

    @pl.when((n_valid > 0) & (first_ref[ic] == 1))
    def _():
        slot = (group_ref[ic] + n * ngroups_ref[0]) & 1
        mine = copies(blk_e_ref[ic], slot, n)

        @pl.when((i == 0) & (n == 0))
        def _():
            for c in mine:
                c.start()

        for c in mine:
            c.wait()
        nxt = next_ref[ic]

        @pl.when(nxt >= 0)
        def _():
            for c in copies(nxt, 1 - slot, n):
                c.start()

        @pl.when((nxt < 0) & (n + 1 < n_tiles))
        def _():
            for c in copies(blk_e_ref[0], 1 - slot, n + 1):
                c.start()

        for k, w_sc in enumerate(w_scs):
            w_sc[...] = wbuf[slot, k].astype(BF16)

    return n_valid


def _moe_gu_kernel(*refs, sub, layer, tn):
    meta = refs[:7]
    x_ref, wg_hbm, wu_hbm, bg_ref, bu_ref, h_ref, wbuf, wg_sc, wu_sc, sems = refs[7:]
    n_valid = _stream_expert_weights(meta, (wg_hbm, wu_hbm), wbuf, (wg_sc, wu_sc), sems, layer, tn)

    for r0 in range(0, x_ref.shape[0], sub):
        @pl.when(n_valid > r0)
        def _(r0=r0):
            x = _unpack_bf16_pairs(x_ref[r0:r0 + sub, :])
            gt = jnp.minimum(jnp.dot(x, wg_sc[...], preferred_element_type=F32) + bg_ref[...], SWIGLU_LIMIT)
            up = jnp.clip(jnp.dot(x, wu_sc[...], preferred_element_type=F32) + bu_ref[...],
                          -SWIGLU_LIMIT, SWIGLU_LIMIT)
            h_ref[r0:r0 + sub, :] = (gt * jax.nn.sigmoid(SWIGLU_ALPHA * gt) * (up + 1.0)).astype(BF16)

        @pl.when(n_valid <= r0)
        def _(r0=r0):
            h_ref[r0:r0 + sub, :] = jnp.zeros((sub, h_ref.shape[1]), h_ref.dtype)


def _moe_down_kernel(*refs, sub, layer, tn):
    meta = refs[:7]
    h_ref, wd_hbm, bd_ref, y_ref, wbuf, wd_sc, sems = refs[7:]
    n_valid = _stream_expert_weights(meta, (wd_hbm,), wbuf, (wd_sc,), sems, layer, tn)

    for r0 in range(0, h_ref.shape[0], sub):
        @pl.when(n_valid > r0)
        def _(r0=r0):
            y_ref[r0:r0 + sub, :] = (jnp.dot(h_ref[r0:r0 + sub, :], wd_sc[...], preferred_element_type=F32)
                                     + bd_ref[...])

        @pl.when(n_valid <= r0)
        def _(r0=r0):
            y_ref[r0:r0 + sub, :] = jnp.zeros((sub, y_ref.shape[1]), y_ref.dtype)


def _moe_experts(xs, meta, layer, wg, bg, wu, bu, wd, bd, tmb, sub, tn_up, tn_down):
    rows, d_packed = xs.shape
    d = 2 * d_packed
    _, n_exp, _, de = wg.shape
    nblk = rows // tmb
    clamp = lambda i, na: jnp.minimum(i, na[0] - 1)
    lhs = lambda w: pl.BlockSpec((tmb, w), lambda n, i, be, na, *_: (clamp(i, na), 0))
    bspec = lambda tn: pl.BlockSpec((None, 1, tn), lambda n, i, be, na, *_: (be[clamp(i, na)], 0, n))
    out = lambda tn: pl.BlockSpec((tmb, tn), lambda n, i, *_: (i, n))
    hbm = pl.BlockSpec(memory_space=pl.ANY)
    bg, bu, bd = bg[layer], bu[layer], bd[layer]
    hidden = pl.pallas_call(
        functools.partial(_moe_gu_kernel, sub=sub, layer=layer, tn=tn_up),
        grid_spec=pltpu.PrefetchScalarGridSpec(
            num_scalar_prefetch=len(meta), grid=(de // tn_up, nblk),
            in_specs=[lhs(d_packed), hbm, hbm, bspec(tn_up), bspec(tn_up)], out_specs=out(tn_up),
            scratch_shapes=[pltpu.VMEM((2, 2, d, tn_up), F32), pltpu.VMEM((d, tn_up), BF16),
                            pltpu.VMEM((d, tn_up), BF16), pltpu.SemaphoreType.DMA((2, 2))]),
        out_shape=jax.ShapeDtypeStruct((rows, de), BF16),
        compiler_params=_params(2),
        name="moe_gate_up",
    )(*meta, xs, wg, wu, bg.reshape(n_exp, 1, de), bu.reshape(n_exp, 1, de))
    return pl.pallas_call(
        functools.partial(_moe_down_kernel, sub=sub, layer=layer, tn=tn_down),
        grid_spec=pltpu.PrefetchScalarGridSpec(
            num_scalar_prefetch=len(meta), grid=(d // tn_down, nblk),
            in_specs=[lhs(de), hbm, bspec(tn_down)], out_specs=out(tn_down),
            scratch_shapes=[pltpu.VMEM((2, 1, de, tn_down), F32), pltpu.VMEM((de, tn_down), BF16),
                            pltpu.SemaphoreType.DMA((2, 1))]),
        out_shape=jax.ShapeDtypeStruct((rows, d), F32),
        compiler_params=_params(2),
        name="moe_down",
    )(*meta, hidden, wd, bd.reshape(n_exp, 1, d))


def _tile(n, pref):
    return pref if n % pref == 0 else n


def _token_mixers(x, is_prompt, batch, seq, lw, lam_init, sample_state):
    (norm1_g, w_in, w_dw, b_dw, conv_ln_g, conv_ln_b, w_conv_out, b_conv_out, lams, subln_g, w_attn_out, w_o,
     norm2_g) = lw
    t, d = x.shape
    dc = w_dw.shape[1]
    da = N_HEADS * V_DIM
    col_q, col_k, col_v = 2 * dc, 2 * dc + da, 2 * dc + 2 * da
    col_gc, col_ga = 2 * dc + 3 * da, 2 * dc + 3 * da + d
    v_new_of = lambda z: z[:, col_v:col_gc].reshape(batch, seq, N_HEADS, V_DIM)
    if is_prompt:
        z, kt = _inproj(x, norm1_g, w_in, _tile(seq, 512), da, kt=(batch, seq, col_k))
        k_new = jnp.transpose(kt.reshape(batch, 2 * N_HEADS, HEAD_DIM, seq), (0, 3, 1, 2))
        v_new = v_new_of(z)
        u, c_act = _conv_prompt(z, batch, seq, dc, w_dw, b_dw, conv_ln_g, conv_ln_b, _tile(seq, 256))
        conv_new = u.reshape(batch, seq, dc)[:, seq - (CONV_WIDTH - 1):]
        o_norm = _attn_prompt(z, batch, seq, col_q, col_k, col_v, lams, subln_g, lam_init, _tile(seq, 512))
    else:
        state_conv, cache_k, cache_v, layer, page_table, past_len = sample_state
        z, = _inproj(x, norm1_g, w_in, t, 1536)
        k_new = z[:, col_k:col_v].reshape(batch, seq, 2 * N_HEADS, HEAD_DIM)
        v_new = v_new_of(z)
        u, c_act = _conv_sample(z, state_conv, dc, w_dw, b_dw, conv_ln_g, conv_ln_b)
        conv_new = jnp.concatenate([state_conv[:, 1:], u[:, None, :]], axis=1)
        vn_dup = jnp.repeat(v_new.reshape(t, N_HEADS, V_DIM), 2, axis=1)
        cache_kt = jnp.transpose(cache_k, (0, 1, 3, 4, 2))
        o_maps = _attn_decode(z[:, col_q:col_k].reshape(t, 1, da), z[:, col_k:col_v].reshape(t, 1, da), vn_dup,
                              cache_kt, cache_v, layer, page_table, past_len)
        o_norm = _diff_norm(o_maps.reshape(t * N_HEADS, 2 * V_DIM), lams, subln_g, lam_init).reshape(t, da)
    x1, h2 = _merge(x, c_act, o_norm, z, col_gc, col_ga, w_conv_out, b_conv_out, w_attn_out, w_o, norm2_g,
                    _tile(t, 256))
    return x1, h2, k_new, v_new, conv_new


def _moe(groups, w_router, b_router, layer, wg, bg, wu, bu, wd, bd, final_g, normalize):
    _, n_exp, d, _ = wg.shape
    tmb, sub = MOE_ROW_BLOCK, MOE_ROW_SUBBLOCK
    wr_t = w_router.T.astype(BF16)
    base = jnp.zeros((n_exp, 1), F32)
    routed = []
    for x1, h2 in groups:
        top_i, top_w, rank, base = _router(h2, wr_t, b_router, base, _tile(h2.shape[0], 512))
        routed.append((top_i, top_w, rank))
    counts = base[:, 0].astype(jnp.int32)
    padded = (counts + tmb - 1) // tmb * tmb
    pend = jnp.cumsum(padded)
    pstart = pend - padded
    n_assign = sum(h2.shape[0] for _, h2 in groups) * TOP_K
    nblk = pl.cdiv(n_assign, tmb) + n_exp
    nact = (pend[-1] // tmb).astype(jnp.int32).reshape(1)
    blk_row = jnp.arange(nblk, dtype=jnp.int32) * tmb
    blk_e = jnp.minimum(jnp.sum(pend[None, :] <= blk_row[:, None], axis=1), n_exp - 1).astype(jnp.int32)
    experts = jnp.arange(n_exp, dtype=jnp.int32)
    blk_last = jnp.sum(jnp.where(blk_e[:, None] == experts[None, :], (pstart + counts)[None, :], 0), axis=1)
    blk_valid = jnp.clip(blk_last - blk_row, 0, tmb).astype(jnp.int32)
    blk_id = jnp.arange(nblk, dtype=jnp.int32)
    active = blk_id < nact[0]
    blk_first = (active & (blk_e != jnp.concatenate([jnp.full((1,), -1, jnp.int32), blk_e[:-1]]))).astype(jnp.int32)
    blk_group = jnp.cumsum(blk_first) - 1
    later_other = active[None, :] & (blk_id[None, :] > blk_id[:, None]) & (blk_e[None, :] != blk_e[:, None])
    nxt_blk = jnp.min(jnp.where(later_other, blk_id[None, :], nblk), axis=1)
    blk_next = jnp.sum(jnp.where(blk_id[None, :] == nxt_blk[:, None], blk_e[None, :] + 1, 0), axis=1) - 1
    meta = (blk_e, nact, blk_valid, blk_first, blk_group.astype(jnp.int32), blk_next.astype(jnp.int32),
            jnp.sum(blk_first).astype(jnp.int32).reshape(1))
    xs = jnp.zeros((nblk * tmb, d // 2), jnp.uint32)
    dests = []
    for (x1, h2), (top_i, top_w, rank) in zip(groups, routed):
        first = jnp.sum(jnp.where(top_i[None] == experts[:, None, None], pstart[:, None, None], 0), axis=0)
        dest = first + rank
        dests.append(dest)
        xs = _dispatch(h2, dest, xs, _tile(h2.shape[0], 256))
    ys = _moe_experts(xs, meta, layer, wg, bg, wu, bu, wd, bd, tmb, sub, MOE_UP_COLS, MOE_DOWN_COLS)
    outs = []
    for (x1, h2), (top_i, top_w, rank), dest in zip(groups, routed, dests):
        outs.append(_combine(x1, dest, top_w, ys, final_g, normalize, _tile(x1.shape[0], 256)))
    return outs


def kernel(x_prompt, x_sample, cache_k, cache_v, state_conv, page_table, norm1_g, w_in, w_dw, b_dw, conv_ln_g, conv_ln_b, w_conv_out, b_conv_out, lambda_q1, lambda_k1, lambda_q2, lambda_k2, subln_g, w_attn_out, w_o, norm2_g, w_router, b_router, w_exp_gate, b_exp_gate, w_exp_up, b_exp_up, w_exp_down, b_exp_down, final_norm_g):
    depth = w_in.shape[0]
    batch, seq, d = x_prompt.shape
    dec_b, dec_seq, _ = x_sample.shape
    assert dec_seq == 1
    past_len = page_table.shape[1] * cache_k.shape[2]
    xp = x_prompt.reshape(batch * seq, d)
    xs = x_sample.reshape(dec_b * dec_seq, d)
    outs = [[] for _ in range(6)]
    for l in range(depth):
        lam_init = 0.8 - 0.6 * math.exp(-0.3 * l)
        lams = (lambda_q1[l], lambda_k1[l], lambda_q2[l], lambda_k2[l])
        lw = (norm1_g[l], w_in[l].astype(BF16), w_dw[l], b_dw[l], conv_ln_g[l], conv_ln_b[l],
              w_conv_out[l].astype(BF16), b_conv_out[l], lams, subln_g[l], w_attn_out[l].astype(BF16),
              w_o[l].astype(BF16), norm2_g[l])
        x1p, h2p, kp, vp, cp = _token_mixers(xp, True, batch, seq, lw, lam_init, None)
        x1s, h2s, ks, vs, cs = _token_mixers(xs, False, dec_b, dec_seq, lw, lam_init,
                                             (state_conv[l], cache_k, cache_v, l, page_table, past_len))
        xp, xs = _moe([(x1p, h2p), (x1s, h2s)], w_router[l], b_router[l], l, w_exp_gate, b_exp_gate,
                      w_exp_up, b_exp_up, w_exp_down, b_exp_down, final_norm_g, l == depth - 1)
        for lst, v in zip(outs, (kp, vp, cp, ks, vs, cs)):
            lst.append(v)
    y_prompt = xp.reshape(batch, seq, d)
    y_sample = xs.reshape(dec_b, dec_seq, d)
    return (y_prompt, y_sample) + tuple(jnp.stack(v) for v in outs)
```

```python
import functools
import math

import jax
import jax.numpy as jnp
from jax import lax
from jax.experimental import pallas as pl
from jax.experimental.pallas import tpu as pltpu

RMS_EPS = 1e-5
LN_EPS = 1e-5
CONV_WIDTH = 31
CONV_HALO = 32
N_HEADS = 8
HEAD_DIM = 64
V_DIM = 128
HEAD_SHIFT = 6
TOP_K = 4
SWIGLU_ALPHA = 1.702
SWIGLU_LIMIT = 7.0
DECODE_PAGES_PER_STEP = 8
MOE_ROW_BLOCK = 512
MOE_ROW_SUBBLOCK = 256
MOE_UP_COLS = 1024
MOE_DOWN_COLS = 2048
NEG_BIG = -1e30

V7X_VMEM_LIMIT_BYTES = 56 * 1024 * 1024
V7X_LANES = 128
V7X_SUBLANES = 8
BF16 = jnp.bfloat16
F32 = jnp.float32


def _params(n_axes):
    return pltpu.CompilerParams(dimension_semantics=("arbitrary",) * n_axes,
                                vmem_limit_bytes=V7X_VMEM_LIMIT_BYTES)


def _rms(x, g):
    return x * lax.rsqrt(jnp.mean(x * x, axis=-1, keepdims=True) + RMS_EPS) * g


def _lam_value(lq1, lk1, lq2, lk2, lam_init):
    s1 = jnp.sum(lq1 * lk1, axis=-1, keepdims=True)
    s2 = jnp.sum(lq2 * lk2, axis=-1, keepdims=True)
    return jnp.exp(s1) - jnp.exp(s2) + lam_init


def _inproj_kernel(x_ref, g_ref, w_ref, z_ref, *rest, kt_tile):
    @pl.when(pl.program_id(1) == 0)
    def _():
        rest[-1][...] = _rms(x_ref[...], g_ref[...]).astype(BF16)

    z = jnp.dot(rest[-1][...], w_ref[...], preferred_element_type=F32)
    z_ref[...] = z
    if kt_tile is not None:
        @pl.when(pl.program_id(1) == kt_tile)
        def _():
            rest[0][...] = z.T


def _inproj(x, g, w_bf16, tm, tn, kt=None):
    t, d = x.shape
    n = w_bf16.shape[1]
    out_specs = [pl.BlockSpec((tm, tn), lambda i, j: (i, j))]
    out_shape = [jax.ShapeDtypeStruct((t, n), F32)]
    kt_tile = None
    if kt is not None:
        batch, seq, col_k = kt
        per_seq = seq // tm
        kt_tile = col_k // tn
        out_specs.append(pl.BlockSpec((None, tn, tm), lambda i, j: (i // per_seq, 0, i % per_seq)))
        out_shape.append(jax.ShapeDtypeStruct((batch, tn, seq), F32))
    return pl.pallas_call(
        functools.partial(_inproj_kernel, kt_tile=kt_tile),
        grid=(t // tm, n // tn),
        in_specs=[pl.BlockSpec((tm, d), lambda i, j: (i, 0)),
                  pl.BlockSpec((1, d), lambda i, j: (0, 0)),
                  pl.BlockSpec((d, tn), lambda i, j: (0, j))],
        out_specs=out_specs,
        out_shape=out_shape,
        scratch_shapes=[pltpu.VMEM((tm, d), BF16)],
        compiler_params=_params(2),
        name="inproj",
    )(x, g.reshape(1, d), w_bf16)


def _ln_silu(y, lng, lnb):
    mu = jnp.mean(y, axis=-1, keepdims=True)
    yc = y - mu
    var = jnp.mean(yc * yc, axis=-1, keepdims=True)
    yn = yc * lax.rsqrt(var + LN_EPS) * lng + lnb
    return yn * jax.nn.sigmoid(yn)


def _conv_prompt_kernel(a_ref, g_ref, wdw_ref, bdw_ref, lng_ref, lnb_ref, u_ref, c_ref, win_sc, shift_sc, *, rows):
    ts = a_ref.shape[0]
    sub = shift_sc.shape[0]

    @pl.when(pl.program_id(1) == 0)
    def _():
        win_sc[0:CONV_HALO, :] = jnp.zeros((CONV_HALO, win_sc.shape[1]), F32)

    u = a_ref[...] * jax.nn.sigmoid(g_ref[...])
    u_ref[...] = u
    win_sc[CONV_HALO:CONV_HALO + ts, :] = u
    span = shift_sc.shape[1]
    for r in range(1, sub):
        shift_sc[r] = win_sc[r:r + span, :]
    off = CONV_HALO - (CONV_WIDTH - 1)
    for r0 in range(0, ts, rows):
        acc = jnp.zeros((rows, u.shape[1]), F32)
        for j in range(CONV_WIDTH):
            whole, r = divmod(off + j, sub)
            lo = r0 + whole * sub
            tap = win_sc[lo:lo + rows, :] if r == 0 else shift_sc[r, lo:lo + rows, :]
            acc = acc + wdw_ref[j:j + 1, :] * tap
        y = acc + bdw_ref[...]
        c_ref[r0:r0 + rows, :] = _ln_silu(y, lng_ref[...], lnb_ref[...]).astype(BF16)
    win_sc[0:CONV_HALO, :] = win_sc[ts:ts + CONV_HALO, :]


def _conv_prompt(z, batch, seq, dc, w_dw, b_dw, ln_g, ln_b, ts):
    t = z.shape[0]
    nt = seq // ts
    vec = lambda v: v.reshape(1, dc)
    const = lambda shape: pl.BlockSpec(shape, lambda b, i: (0, 0))
    return pl.pallas_call(
        functools.partial(_conv_prompt_kernel, rows=32),
        grid=(batch, nt),
        in_specs=[pl.BlockSpec((ts, dc), lambda b, i: (b * nt + i, 0)),
                  pl.BlockSpec((ts, dc), lambda b, i: (b * nt + i, 1)),
                  const((CONV_WIDTH, dc)), const((1, dc)), const((1, dc)), const((1, dc))],
        out_specs=[pl.BlockSpec((ts, dc), lambda b, i: (b * nt + i, 0)),
                   pl.BlockSpec((ts, dc), lambda b, i: (b * nt + i, 0))],
        out_shape=[jax.ShapeDtypeStruct((t, dc), F32), jax.ShapeDtypeStruct((t, dc), BF16)],
        scratch_shapes=[pltpu.VMEM((ts + CONV_HALO, dc), F32),
                        pltpu.VMEM((V7X_SUBLANES, ts + CONV_HALO - V7X_SUBLANES, dc), F32)],
        compiler_params=_params(2),
        name="conv_prompt",
    )(z, z, w_dw, vec(b_dw), vec(ln_g), vec(ln_b))


def _conv_sample_kernel(a_ref, g_ref, st_ref, wdw_ref, bdw_ref, lng_ref, lnb_ref, u_ref, c_ref):
    nb = a_ref.shape[0]
    hist = CONV_WIDTH - 1
    u = a_ref[...] * jax.nn.sigmoid(g_ref[...])
    u_ref[...] = u

    def body(b, carry):
        past = jnp.sum(st_ref[b] * wdw_ref[0:hist, :], axis=0, keepdims=True)
        ub = u_ref[pl.ds(b, 1), :]
        y = past + ub * wdw_ref[hist:hist + 1, :] + bdw_ref[...]
        c_ref[pl.ds(b, 1), :] = _ln_silu(y, lng_ref[...], lnb_ref[...])
        return carry

    lax.fori_loop(0, nb, body, 0)


def _conv_sample(z, state, dc, w_dw, b_dw, ln_g, ln_b):
    nb = z.shape[0]
    vec = lambda v: v.reshape(1, dc)
    const = lambda shape: pl.BlockSpec(shape, lambda i: (0,) * len(shape))
    return pl.pallas_call(
        _conv_sample_kernel,
        grid=(1,),
        in_specs=[pl.BlockSpec((nb, dc), lambda i: (0, 0)),
                  pl.BlockSpec((nb, dc), lambda i: (0, 1)),
                  const(state.shape), const((CONV_WIDTH, dc)), const((1, dc)), const((1, dc)), const((1, dc))],
        out_specs=[const((nb, dc)), const((nb, dc))],
        out_shape=[jax.ShapeDtypeStruct((nb, dc), F32), jax.ShapeDtypeStruct((nb, dc), F32)],
        compiler_params=_params(1),
        name="conv_sample",
    )(z, z, state, w_dw, vec(b_dw), vec(ln_g), vec(ln_b))


def _attn_prompt_kernel(slopes_ref, q_ref, k_ref, v_ref, lq1_ref, lk1_ref, lq2_ref, lk2_ref, g_ref, o_ref,
                        vt_sc, kb_sc, acc_sc, *, lam_init):
    tq = q_ref.shape[0]
    n_kv = vt_sc.shape[0]
    h = pl.program_id(1)
    qi = pl.program_id(2)
    slope = slopes_ref[h]
    lam = _lam_value(lq1_ref[...], lk1_ref[...], lq2_ref[...], lk2_ref[...], lam_init)

    @pl.when(qi == 0)
    def _():
        for jb in range(n_kv):
            vt_sc[jb] = v_ref[jb * tq:(jb + 1) * tq, :].T.astype(BF16)
        kb_sc[...] = slope * lax.broadcasted_iota(jnp.int32, kb_sc.shape, 0).astype(F32)

    q = (q_ref[...] * (HEAD_DIM ** -0.5)).astype(BF16)
    lane = lax.broadcasted_iota(jnp.int32, q.shape, 1)
    zero = jnp.zeros_like(q)
    qq = jnp.concatenate([jnp.where(lane < HEAD_DIM, q, zero), jnp.where(lane >= HEAD_DIM, q, zero)], axis=0)
    acc_sc[...] = jnp.zeros(acc_sc.shape, F32)

    def step(j, carry, diagonal):
        m, l = carry
        kb = k_ref[pl.ds(pl.multiple_of(j * tq, tq), tq), :].astype(BF16)
        s = lax.dot_general(kb, qq, (((1,), (1,)), ((), ())), preferred_element_type=F32)
        bias = kb_sc[...]
        s = s + jnp.concatenate([bias] * (2 * tq // bias.shape[1]), axis=1)
        if diagonal:
            r = lax.broadcasted_iota(jnp.int32, s.shape, 0)
            c = lax.broadcasted_iota(jnp.int32, s.shape, 1)
            c = jnp.where(c >= tq, c - tq, c)
            s = jnp.where(r <= c, s, -jnp.inf)
        off = slope * ((j - qi) * tq).astype(F32)
        m_new = jnp.maximum(m, jnp.max(s, axis=0, keepdims=True) + off)
        alpha = jnp.exp(m - m_new)
        p = jnp.exp(s - (m_new - off))
        l = alpha * l + jnp.sum(p, axis=0, keepdims=True)
        acc_sc[...] = alpha * acc_sc[...] + jnp.dot(vt_sc[j], p.astype(BF16), preferred_element_type=F32)
        return m_new, l

    init = (jnp.full((1, 2 * tq), NEG_BIG, F32), jnp.zeros((1, 2 * tq), F32))
    carry = lax.fori_loop(0, qi, lambda j, c: step(j, c, False), init)
    _, l = step(qi, carry, True)
    o = acc_sc[...] / l
    d = o[:, :tq] - lam * o[:, tq:]
    dn = d * lax.rsqrt(jnp.mean(d * d, axis=0, keepdims=True) + RMS_EPS)
    o_ref[...] = (dn.T * g_ref[...] * (1.0 - lam_init)).astype(BF16)


def _attn_prompt(z, batch, seq, col_q, col_k, col_v, lams, subln_g, lam_init, tq):
    t = z.shape[0]
    nq = seq // tq
    slopes = 2.0 ** (-8.0 * jnp.arange(1, N_HEADS + 1, dtype=F32) / N_HEADS)
    blk = 2 * HEAD_DIM
    lam_spec = pl.BlockSpec((1, HEAD_DIM), lambda b, h, i, s: (0, 0))
    grid_spec = pltpu.PrefetchScalarGridSpec(
        num_scalar_prefetch=1,
        grid=(batch, N_HEADS, nq),
        in_specs=[pl.BlockSpec((tq, blk), lambda b, h, i, s: (b * nq + i, col_q // blk + h)),
                  pl.BlockSpec((seq, blk), lambda b, h, i, s: (b, col_k // blk + h)),
                  pl.BlockSpec((seq, V_DIM), lambda b, h, i, s: (b, col_v // V_DIM + h)),
                  lam_spec, lam_spec, lam_spec, lam_spec,
                  pl.BlockSpec((1, V_DIM), lambda b, h, i, s: (0, 0))],
        out_specs=pl.BlockSpec((tq, V_DIM), lambda b, h, i, s: (b * nq + i, h)),
        scratch_shapes=[pltpu.VMEM((nq, V_DIM, tq), BF16), pltpu.VMEM((tq, V7X_LANES), F32),
                        pltpu.VMEM((V_DIM, 2 * tq), F32)],
    )
    return pl.pallas_call(
        functools.partial(_attn_prompt_kernel, lam_init=lam_init),
        grid_spec=grid_spec,
        out_shape=jax.ShapeDtypeStruct((t, N_HEADS * V_DIM), BF16),
        compiler_params=_params(3),
        name="attn_prompt",
    )(slopes, z, z, z, *[v.reshape(1, HEAD_DIM) for v in lams], subln_g.reshape(1, V_DIM))


def _attn_decode_kernel(pt_ref, q_ref, kn_ref, vn_ref, spread_ref, *refs, past_len, pages):
    kc_refs, vc_refs = refs[:pages], refs[pages:2 * pages]
    o_ref, m_sc, l_sc, acc_sc, v_sc = refs[2 * pages:]
    p = pl.program_id(1)
    n_steps = pl.num_programs(1)
    n_maps = 2 * N_HEADS
    keys = kc_refs[0].shape[2]

    @pl.when(p == 0)
    def _():
        m_sc[...] = jnp.full(m_sc.shape, NEG_BIG, F32)
        l_sc[...] = jnp.zeros(l_sc.shape, F32)
        acc_sc[...] = jnp.zeros(acc_sc.shape, F32)

    q_row = q_ref[...] * (HEAD_DIM ** -0.5)
    qrow_i = lax.broadcasted_iota(jnp.int32, (n_maps, n_maps * HEAD_DIM), 0)
    qcol_i = lax.broadcasted_iota(jnp.int32, (n_maps, n_maps * HEAD_DIM), 1)
    qbd = jnp.where((qcol_i >> HEAD_SHIFT) == qrow_i, q_row, 0.0).astype(BF16)
    row1 = lax.broadcasted_iota(jnp.int32, (n_maps, 1), 0)
    slope = jnp.exp2(-((row1 >> 1) + 1).astype(F32))

    s = jnp.concatenate(
        [jnp.dot(qbd, kc_ref[...].reshape(n_maps * HEAD_DIM, keys).astype(BF16), preferred_element_type=F32)
         for kc_ref in kc_refs], axis=1)
    k_pos = p * (pages * keys) + lax.broadcasted_iota(jnp.int32, s.shape, 1)
    s = s - slope * (past_len - k_pos).astype(F32)

    m_prev = m_sc[...]
    m_new = jnp.maximum(m_prev, jnp.max(s, axis=1, keepdims=True))
    alpha = jnp.exp(m_prev - m_new)
    pe = jnp.exp(s - m_new)
    l_sc[...] = alpha * l_sc[...] + jnp.sum(pe, axis=1, keepdims=True)
    m_sc[...] = m_new

    pb = pe.astype(BF16)
    stacked = jnp.concatenate([pb[:, i * keys:(i + 1) * keys] for i in range(pages)], axis=0)
    spread = jnp.dot(stacked, spread_ref[...], preferred_element_type=F32)
    srow = lax.broadcasted_iota(jnp.int32, spread.shape, 0)
    scol = lax.broadcasted_iota(jnp.int32, spread.shape, 1)
    own_head = (scol & (N_HEADS - 1)) == ((srow & (n_maps - 1)) >> 1)
    pc = jnp.where(own_head, spread, 0.0).astype(BF16)
    pc = jnp.concatenate([pc[i * n_maps:(i + 1) * n_maps, :] for i in range(pages)], axis=1)
    rows = keys * N_HEADS
    for i, vc_ref in enumerate(vc_refs):
        v_sc[i * rows:(i + 1) * rows, :] = vc_ref[...].reshape(rows, V_DIM).astype(BF16)
    acc_sc[...] = alpha * acc_sc[...] + jnp.dot(pc, v_sc[...], preferred_element_type=F32)

    @pl.when(p == n_steps - 1)
    def _():
        kn = kn_ref[...].astype(BF16).astype(F32)
        s_new = jnp.sum(qbd.astype(F32) * kn, axis=1, keepdims=True)
        m_old = m_sc[...]
        m_fin = jnp.maximum(m_old, s_new)
        a = jnp.exp(m_old - m_fin)
        p_new = jnp.exp(s_new - m_fin)
        l_fin = a * l_sc[...] + p_new
        vn = vn_ref[...].astype(BF16).astype(F32)
        acc = a * acc_sc[...] + p_new.astype(BF16).astype(F32) * vn
        o_ref[...] = acc / l_fin


def _attn_decode(q_s, kn_s, vn_dup, cache_kt, cache_v, layer, page_table, past_len):
    nb, n_pages = page_table.shape
    n_maps = 2 * N_HEADS
    page = cache_kt.shape[-1]
    pages = DECODE_PAGES_PER_STEP if n_pages % DECODE_PAGES_PER_STEP == 0 else 1
    spread = (jnp.arange(page)[:, None] == jnp.arange(page * N_HEADS)[None, :] // N_HEADS).astype(BF16)

    def page_spec(shape, i):
        return pl.BlockSpec((None, None) + shape,
                            lambda b, p, pt: (layer, pt[b * n_pages + p * pages + i], 0, 0, 0))

    grid_spec = pltpu.PrefetchScalarGridSpec(
        num_scalar_prefetch=1,
        grid=(nb, n_pages // pages),
        in_specs=[pl.BlockSpec((None, 1, n_maps * HEAD_DIM), lambda b, p, pt: (b, 0, 0)),
                  pl.BlockSpec((None, 1, n_maps * HEAD_DIM), lambda b, p, pt: (b, 0, 0)),
                  pl.BlockSpec((None, n_maps, V_DIM), lambda b, p, pt: (b, 0, 0)),
                  pl.BlockSpec((page, page * N_HEADS), lambda b, p, pt: (0, 0))]
                 + [page_spec((n_maps, HEAD_DIM, page), i) for i in range(pages)]
                 + [page_spec((page, N_HEADS, V_DIM), i) for i in range(pages)],
        out_specs=pl.BlockSpec((None, n_maps, V_DIM), lambda b, p, pt: (b, 0, 0)),
        scratch_shapes=[pltpu.VMEM((n_maps, 1), F32), pltpu.VMEM((n_maps, 1), F32),
                        pltpu.VMEM((n_maps, V_DIM), F32), pltpu.VMEM((pages * page * N_HEADS, V_DIM), BF16)],
    )
    return pl.pallas_call(
        functools.partial(_attn_decode_kernel, past_len=past_len, pages=pages),
        grid_spec=grid_spec,
        out_shape=jax.ShapeDtypeStruct((nb, n_maps, V_DIM), F32),
        compiler_params=_params(2),
        name="attn_decode",
    )(page_table.reshape(-1), q_s, kn_s, vn_dup, spread, *([cache_kt] * pages), *([cache_v] * pages))


def _diff_norm_kernel(o_ref, lq1_ref, lk1_ref, lq2_ref, lk2_ref, g_ref, out_ref, *, lam_init):
    lam = _lam_value(lq1_ref[...], lk1_ref[...], lq2_ref[...], lk2_ref[...], lam_init)
    o = o_ref[...]
    d = o[:, :V_DIM] - lam * o[:, V_DIM:]
    out_ref[...] = _rms(d, g_ref[...]) * (1.0 - lam_init)


def _diff_norm(o_pairs, lams, subln_g, lam_init):
    rows = o_pairs.shape[0]
    const = lambda shape: pl.BlockSpec(shape, lambda i: (0, 0))
    return pl.pallas_call(
        functools.partial(_diff_norm_kernel, lam_init=lam_init),
        grid=(1,),
        in_specs=[const(o_pairs.shape)] + [const((1, HEAD_DIM))] * 4 + [const((1, V_DIM))],
        out_specs=const((rows, V_DIM)),
        out_shape=jax.ShapeDtypeStruct((rows, V_DIM), F32),
        compiler_params=_params(1),
        name="diff_norm",
    )(o_pairs, *[v.reshape(1, HEAD_DIM) for v in lams], subln_g.reshape(1, V_DIM))


def _pack_bf16_pairs(x):
    n = x.shape[1] // 2
    bits = lax.bitcast_convert_type(x.astype(BF16).astype(F32), jnp.uint32)
    return (bits[:, :n] >> 16) | (bits[:, n:] & jnp.uint32(0xFFFF0000))


def _unpack_bf16_pairs(w):
    lo = lax.bitcast_convert_type(w << 16, F32)
    hi = lax.bitcast_convert_type(w & jnp.uint32(0xFFFF0000), F32)
    return jnp.concatenate([lo, hi], axis=1).astype(BF16)


def _merge_kernel(x_ref, c_ref, o_ref, gc0_ref, gc1_ref, ga0_ref, ga1_ref, wco_ref, bco_ref, wao_ref, wo_ref,
                  g2_ref, x1_ref, h2_ref):
    half = gc0_ref.shape[1]
    conv_out = jnp.dot(c_ref[...].astype(BF16), wco_ref[...], preferred_element_type=F32) + bco_ref[...]
    attn_out = jnp.dot(o_ref[...].astype(BF16), wao_ref[...], preferred_element_type=F32)
    mixed = []
    for n, (gc_ref, ga_ref) in enumerate(((gc0_ref, ga0_ref), (gc1_ref, ga1_ref))):
        sl = slice(n * half, (n + 1) * half)
        mixed.append((jax.nn.sigmoid(gc_ref[...]) * conv_out[:, sl]
                      + jax.nn.sigmoid(ga_ref[...]) * attn_out[:, sl]).astype(BF16))
    x1 = x_ref[...] + jnp.dot(jnp.concatenate(mixed, axis=1), wo_ref[...], preferred_element_type=F32)
    x1_ref[...] = x1
    h2_ref[...] = _pack_bf16_pairs(_rms(x1, g2_ref[...]))


def _merge(x, c_act, o_norm, z, col_gc, col_ga, wco, bco, wao, wo, g2, tm):
    t, d = x.shape
    dc, da = c_act.shape[1], o_norm.shape[1]
    half = d // 2
    row = lambda w: pl.BlockSpec((tm, w), lambda i: (i, 0))
    gate = lambda blk: pl.BlockSpec((tm, half), lambda i: (i, blk))
    weight = lambda shape: pl.BlockSpec(shape, lambda i: (0, 0), pipeline_mode=pl.Buffered(1))
    return pl.pallas_call(
        _merge_kernel,
        grid=(t // tm,),
        in_specs=[row(d), row(dc), row(da),
                  gate(col_gc // half), gate(col_gc // half + 1), gate(col_ga // half), gate(col_ga // half + 1),
                  weight((dc, d)), weight((1, d)), weight((da, d)), weight((d, d)), weight((1, d))],
        out_specs=[row(d), row(half)],
        out_shape=[jax.ShapeDtypeStruct((t, d), F32), jax.ShapeDtypeStruct((t, half), jnp.uint32)],
        compiler_params=_params(1),
        name="merge",
    )(x, c_act, o_norm, z, z, z, z, wco, bco.reshape(1, d), wao, wo, g2.reshape(1, d))


def _router_kernel(h_ref, wr_ref, br_ref, base_ref, tri_ref, ti_ref, tw_ref, rk_ref, cnt_ref, cnt_sc, *, n_tokens):
    i = pl.program_id(0)
    tb = h_ref.shape[0]
    n_exp = wr_ref.shape[0]

    @pl.when(i == 0)
    def _():
        cnt_sc[...] = base_ref[...]

    logits = lax.dot_general(wr_ref[...], _unpack_bf16_pairs(h_ref[...]), (((1,), (1,)), ((), ())),
                             preferred_element_type=F32) + br_ref[...]
    tok = i * tb + lax.broadcasted_iota(jnp.int32, (1, tb), 1)
    valid = tok < n_tokens
    eidx = lax.broadcasted_iota(jnp.int32, (n_exp, tb), 0)
    run = cnt_sc[...]
    x = logits
    vals = []
    for k in range(TOP_K):
        mx = jnp.max(x, axis=0, keepdims=True)
        sel = jnp.min(jnp.where(x == mx, eidx, n_exp), axis=0, keepdims=True)
        hit = eidx == sel
        x = jnp.where(hit, -jnp.inf, x)
        onehot = jnp.where(hit & valid, 1.0, 0.0)
        before = jnp.dot(onehot.astype(BF16), tri_ref[...], preferred_element_type=F32)
        rank = jnp.sum(onehot * (before + run), axis=0, keepdims=True)
        run = run + jnp.sum(onehot, axis=1, keepdims=True)
        vals.append(mx)
        ti_ref[k:k + 1, :] = sel
        rk_ref[k:k + 1, :] = rank.astype(jnp.int32)
    es = [jnp.exp(v - vals[0]) for v in vals]
    tot = es[0] + es[1] + es[2] + es[3]
    for k in range(TOP_K):
        tw_ref[k:k + 1, :] = es[k] / tot
    cnt_sc[...] = run
    cnt_ref[...] = run


def _router(h2, wr_t, b_router, base, tb):
    t, d = h2.shape
    n_exp = wr_t.shape[0]
    nb = pl.cdiv(t, tb)
    tri = (jnp.arange(tb)[:, None] < jnp.arange(tb)[None, :]).astype(BF16)
    const = lambda shape: pl.BlockSpec(shape, lambda i: (0, 0))
    tok = lambda dt: jax.ShapeDtypeStruct((TOP_K, t), dt)
    return pl.pallas_call(
        functools.partial(_router_kernel, n_tokens=t),
        grid=(nb,),
        in_specs=[pl.BlockSpec((tb, d), lambda i: (i, 0)), const(wr_t.shape), const((n_exp, 1)),
                  const((n_exp, 1)), const((tb, tb))],
        out_specs=[pl.BlockSpec((TOP_K, tb), lambda i: (0, i))] * 3 + [const((n_exp, 1))],
        out_shape=[tok(jnp.int32), tok(F32), tok(jnp.int32), jax.ShapeDtypeStruct((n_exp, 1), F32)],
        scratch_shapes=[pltpu.VMEM((n_exp, 1), F32)],
        compiler_params=_params(1),
        name="router",
    )(h2, wr_t, b_router.reshape(n_exp, 1), base, tri)


def _block_index_list(dest, tb):
    t = dest.shape[1]
    return dest.reshape(TOP_K, t // tb, tb).transpose(1, 0, 2).reshape(-1)


def _dispatch_kernel(dest_ref, h_ref, xs_in_ref, xs_ref, idx_sm, idx_sem, row_sem):
    del xs_in_ref
    i = pl.program_id(0)
    tb = h_ref.shape[0]
    n_idx = TOP_K * tb
    cp = pltpu.make_async_copy(dest_ref.at[pl.ds(pl.multiple_of(i * n_idx, n_idx), n_idx)], idx_sm, idx_sem)
    cp.start()
    cp.wait()

    def row_copy(t, k):
        d = idx_sm[k * tb + t]
        return pltpu.make_async_copy(h_ref.at[pl.ds(t, 1), :], xs_ref.at[pl.ds(d, 1), :], row_sem)

    _run_row_copies(tb, row_copy)


def _run_row_copies(tb, row_copy):
    def issue(t, carry):
        for k in range(TOP_K):
            row_copy(t, k).start()
        return carry

    def drain(t, carry):
        for k in range(TOP_K):
            row_copy(t, k).wait()
        return carry

    lax.fori_loop(0, tb, issue, 0)
    lax.fori_loop(0, tb, drain, 0)


def _dispatch(h2, dest, xs, tb):
    t, d = h2.shape
    return pl.pallas_call(
        _dispatch_kernel,
        grid=(t // tb,),
        in_specs=[pl.BlockSpec(memory_space=pl.ANY),
                  pl.BlockSpec((tb, d), lambda i: (i, 0)),
                  pl.BlockSpec(memory_space=pl.ANY)],
        out_specs=pl.BlockSpec(memory_space=pl.ANY),
        out_shape=jax.ShapeDtypeStruct(xs.shape, xs.dtype),
        scratch_shapes=[pltpu.SMEM((TOP_K * tb,), jnp.int32), pltpu.SemaphoreType.DMA(()),
                        pltpu.SemaphoreType.DMA(())],
        input_output_aliases={2: 0},
        compiler_params=_params(1),
        name="moe_dispatch",
    )(_block_index_list(dest, tb), h2, xs)


def _combine_kernel(dest_ref, x1_ref, w_ref, g_ref, ys_ref, y_ref, idx_sm, rows_sc, idx_sem, row_sem, *, normalize):
    i = pl.program_id(0)
    tb = x1_ref.shape[0]
    n_idx = TOP_K * tb
    cp = pltpu.make_async_copy(dest_ref.at[pl.ds(pl.multiple_of(i * n_idx, n_idx), n_idx)], idx_sm, idx_sem)
    cp.start()
    cp.wait()

    def row_copy(t, k):
        d = idx_sm[k * tb + t]
        return pltpu.make_async_copy(ys_ref.at[pl.ds(d, 1), :], rows_sc.at[k, pl.ds(t, 1), :], row_sem)

    _run_row_copies(tb, row_copy)
    y = x1_ref[...]
    for k in range(TOP_K):
        y = y + w_ref[:, k:k + 1] * rows_sc[k]
    y_ref[...] = _rms(y, g_ref[...]) if normalize else y


def _combine(x1, dest, top_w, ys, final_g, normalize, tb):
    t, d = x1.shape
    return pl.pallas_call(
        functools.partial(_combine_kernel, normalize=normalize),
        grid=(t // tb,),
        in_specs=[pl.BlockSpec(memory_space=pl.ANY),
                  pl.BlockSpec((tb, d), lambda i: (i, 0)),
                  pl.BlockSpec((tb, TOP_K), lambda i: (i, 0)),
                  pl.BlockSpec((1, d), lambda i: (0, 0)),
                  pl.BlockSpec(memory_space=pl.ANY)],
        out_specs=pl.BlockSpec((tb, d), lambda i: (i, 0)),
        out_shape=jax.ShapeDtypeStruct((t, d), F32),
        scratch_shapes=[pltpu.SMEM((TOP_K * tb,), jnp.int32), pltpu.VMEM((TOP_K, tb, d), F32),
                        pltpu.SemaphoreType.DMA(()), pltpu.SemaphoreType.DMA(())],
        compiler_params=_params(1),
        name="moe_combine",
    )(_block_index_list(dest, tb), x1, top_w.T, final_g.reshape(1, d), ys)


def _block_rows(nact_ref, valid_ref, i):
    nact = nact_ref[0]
    ic = jnp.minimum(i, nact - 1)
    return ic, jnp.where(i < nact, valid_ref[ic], 0)


def _stream_expert_weights(meta, w_hbms, wbuf, w_scs, sems, layer, tn):
    blk_e_ref, nact_ref, valid_ref, first_ref, group_ref, next_ref, ngroups_ref = meta
    n = pl.program_id(0)
    i = pl.program_id(1)
    n_tiles = pl.num_programs(0)
    ic, n_valid = _block_rows(nact_ref, valid_ref, i)

    def copies(expert, slot, tile):
        return [pltpu.make_async_copy(w.at[layer, expert, :, pl.ds(pl.multiple_of(tile * tn, tn), tn)],
                                      wbuf.at[slot, k], sems.at[slot, k]) for k, w in enumerate(w_hbms)]

    @pl.when((n_valid > 0) & (first_ref[ic] == 1))
    def _():
        slot = (group_ref[ic] + n * ngroups_ref[0]) & 1
        mine = copies(blk_e_ref[ic], slot, n)

        @pl.when((i == 0) & (n == 0))
        def _():
            for c in mine:
                c.start()

        for c in mine:
            c.wait()
        nxt = next_ref[ic]

        @pl.when(nxt >= 0)
        def _():
            for c in copies(nxt, 1 - slot, n):
                c.start()

        @pl.when((nxt < 0) & (n + 1 < n_tiles))
        def _():
            for c in copies(blk_e_ref[0], 1 - slot, n + 1):
                c.start()

        for k, w_sc in enumerate(w_scs):
            w_sc[...] = wbuf[slot, k].astype(BF16)

    return n_valid


def _moe_gu_kernel(*refs, sub, layer, tn):
    meta = refs[:7]
    x_ref, wg_hbm, wu_hbm, bg_ref, bu_ref, h_ref, wbuf, wg_sc, wu_sc, sems = refs[7:]
    n_valid = _stream_expert_weights(meta, (wg_hbm, wu_hbm), wbuf, (wg_sc, wu_sc), sems, layer, tn)

    for r0 in range(0, x_ref.shape[0], sub):
        @pl.when(n_valid > r0)
        def _(r0=r0):
            x = _unpack_bf16_pairs(x_ref[r0:r0 + sub, :])
            gt = jnp.minimum(jnp.dot(x, wg_sc[...], preferred_element_type=F32) + bg_ref[...], SWIGLU_LIMIT)
            up = jnp.clip(jnp.dot(x, wu_sc[...], preferred_element_type=F32) + bu_ref[...],
                          -SWIGLU_LIMIT, SWIGLU_LIMIT)
            h_ref[r0:r0 + sub, :] = (gt * jax.nn.sigmoid(SWIGLU_ALPHA * gt) * (up + 1.0)).astype(BF16)

        @pl.when(n_valid <= r0)
        def _(r0=r0):
            h_ref[r0:r0 + sub, :] = jnp.zeros((sub, h_ref.shape[1]), h_ref.dtype)


def _moe_down_kernel(*refs, sub, layer, tn):
    meta = refs[:7]
    h_ref, wd_hbm, bd_ref, y_ref, wbuf, wd_sc, sems = refs[7:]
    n_valid = _stream_expert_weights(meta, (wd_hbm,), wbuf, (wd_sc,), sems, layer, tn)

    for r0 in range(0, h_ref.shape[0], sub):
        @pl.when(n_valid > r0)
        def _(r0=r0):
            y_ref[r0:r0 + sub, :] = (jnp.dot(h_ref[r0:r0 + sub, :], wd_sc[...], preferred_element_type=F32)
                                     + bd_ref[...])

        @pl.when(n_valid <= r0)
        def _(r0=r0):
            y_ref[r0:r0 + sub, :] = jnp.zeros((sub, y_ref.shape[1]), y_ref.dtype)


def _moe_experts(xs, meta, layer, wg, bg, wu, bu, wd, bd, tmb, sub, tn_up, tn_down):
    rows, d_packed = xs.shape
    d = 2 * d_packed
    _, n_exp, _, de = wg.shape
    nblk = rows // tmb
    clamp = lambda i, na: jnp.minimum(i, na[0] - 1)
    lhs = lambda w: pl.BlockSpec((tmb, w), lambda n, i, be, na, *_: (clamp(i, na), 0))
    bspec = lambda tn: pl.BlockSpec((None, 1, tn), lambda n, i, be, na, *_: (be[clamp(i, na)], 0, n))
    out = lambda tn: pl.BlockSpec((tmb, tn), lambda n, i, *_: (i, n))
    hbm = pl.BlockSpec(memory_space=pl.ANY)
    bg, bu, bd = bg[layer], bu[layer], bd[layer]
    hidden = pl.pallas_call(
        functools.partial(_moe_gu_kernel, sub=sub, layer=layer, tn=tn_up),
        grid_spec=pltpu.PrefetchScalarGridSpec(
            num_scalar_prefetch=len(meta), grid=(de // tn_up, nblk),
            in_specs=[lhs(d_packed), hbm, hbm, bspec(tn_up), bspec(tn_up)], out_specs=out(tn_up),
            scratch_shapes=[pltpu.VMEM((2, 2, d, tn_up), F32), pltpu.VMEM((d, tn_up), BF16),
                            pltpu.VMEM((d, tn_up), BF16), pltpu.SemaphoreType.DMA((2, 2))]),
        out_shape=jax.ShapeDtypeStruct((rows, de), BF16),
        compiler_params=_params(2),
        name="moe_gate_up",
    )(*meta, xs, wg, wu, bg.reshape(n_exp, 1, de), bu.reshape(n_exp, 1, de))
    return pl.pallas_call(
        functools.partial(_moe_down_kernel, sub=sub, layer=layer, tn=tn_down),
        grid_spec=pltpu.PrefetchScalarGridSpec(
            num_scalar_prefetch=len(meta), grid=(d // tn_down, nblk),
            in_specs=[lhs(de), hbm, bspec(tn_down)], out_specs=out(tn_down),
            scratch_shapes=[pltpu.VMEM((2, 1, de, tn_down), F32), pltpu.VMEM((de, tn_down), BF16),
                            pltpu.SemaphoreType.DMA((2, 1))]),
        out_shape=jax.ShapeDtypeStruct((rows, d), F32),
        compiler_params=_params(2),
        name="moe_down",
    )(*meta, hidden, wd, bd.reshape(n_exp, 1, d))


def _tile(n, pref):
    return pref if n % pref == 0 else n


def _token_mixers(x, is_prompt, batch, seq, lw, lam_init, sample_state):
    (norm1_g, w_in, w_dw, b_dw, conv_ln_g, conv_ln_b, w_conv_out, b_conv_out, lams, subln_g, w_attn_out, w_o,
     norm2_g) = lw
    t, d = x.shape
    dc = w_dw.shape[1]
    da = N_HEADS * V_DIM
    col_q, col_k, col_v = 2 * dc, 2 * dc + da, 2 * dc + 2 * da
    col_gc, col_ga = 2 * dc + 3 * da, 2 * dc + 3 * da + d
    v_new_of = lambda z: z[:, col_v:col_gc].reshape(batch, seq, N_HEADS, V_DIM)
    if is_prompt:
        z, kt = _inproj(x, norm1_g, w_in, _tile(seq, 1024), da, kt=(batch, seq, col_k))
        k_new = jnp.transpose(kt.reshape(batch, 2 * N_HEADS, HEAD_DIM, seq), (0, 3, 1, 2))
        v_new = v_new_of(z)
        u, c_act = _conv_prompt(z, batch, seq, dc, w_dw, b_dw, conv_ln_g, conv_ln_b, _tile(seq, 256))
        conv_new = u.reshape(batch, seq, dc)[:, seq - (CONV_WIDTH - 1):]
        o_norm = _attn_prompt(z, batch, seq, col_q, col_k, col_v, lams, subln_g, lam_init, _tile(seq, 512))
    else:
        state_conv, cache_k, cache_v, layer, page_table, past_len = sample_state
        z, = _inproj(x, norm1_g, w_in, t, 1536)
        k_new = z[:, col_k:col_v].reshape(batch, seq, 2 * N_HEADS, HEAD_DIM)
        v_new = v_new_of(z)
        u, c_act = _conv_sample(z, state_conv, dc, w_dw, b_dw, conv_ln_g, conv_ln_b)
        conv_new = jnp.concatenate([state_conv[:, 1:], u[:, None, :]], axis=1)
        vn_dup = jnp.repeat(v_new.reshape(t, N_HEADS, V_DIM), 2, axis=1)
        cache_kt = jnp.transpose(cache_k, (0, 1, 3, 4, 2))
        o_maps = _attn_decode(z[:, col_q:col_k].reshape(t, 1, da), z[:, col_k:col_v].reshape(t, 1, da), vn_dup,
                              cache_kt, cache_v, layer, page_table, past_len)
        o_norm = _diff_norm(o_maps.reshape(t * N_HEADS, 2 * V_DIM), lams, subln_g, lam_init).reshape(t, da)
    x1, h2 = _merge(x, c_act, o_norm, z, col_gc, col_ga, w_conv_out, b_conv_out, w_attn_out, w_o, norm2_g,
                    _tile(t, 256))
    return x1, h2, k_new, v_new, conv_new


def _moe(groups, w_router, b_router, layer, wg, bg, wu, bu, wd, bd, final_g, normalize):
    _, n_exp, d, _ = wg.shape
    tmb, sub = MOE_ROW_BLOCK, MOE_ROW_SUBBLOCK
    wr_t = w_router.T.astype(BF16)
    base = jnp.zeros((n_exp, 1), F32)
    routed = []
    for x1, h2 in groups:
        top_i, top_w, rank, base = _router(h2, wr_t, b_router, base, _tile(h2.shape[0], 512))
        routed.append((top_i, top_w, rank))
    counts = base[:, 0].astype(jnp.int32)
    padded = (counts + tmb - 1) // tmb * tmb
    pend = jnp.cumsum(padded)
    pstart = pend - padded
    n_assign = sum(h2.shape[0] for _, h2 in groups) * TOP_K
    nblk = pl.cdiv(n_assign, tmb) + n_exp
    nact = (pend[-1] // tmb).astype(jnp.int32).reshape(1)
    blk_row = jnp.arange(nblk, dtype=jnp.int32) * tmb
    blk_e = jnp.minimum(jnp.sum(pend[None, :] <= blk_row[:, None], axis=1), n_exp - 1).astype(jnp.int32)
    experts = jnp.arange(n_exp, dtype=jnp.int32)
    blk_last = jnp.sum(jnp.where(blk_e[:, None] == experts[None, :], (pstart + counts)[None, :], 0), axis=1)
    blk_valid = jnp.clip(blk_last - blk_row, 0, tmb).astype(jnp.int32)
    blk_id = jnp.arange(nblk, dtype=jnp.int32)
    active = blk_id < nact[0]
    blk_first = (active & (blk_e != jnp.concatenate([jnp.full((1,), -1, jnp.int32), blk_e[:-1]]))).astype(jnp.int32)
    blk_group = jnp.cumsum(blk_first) - 1
    later_other = active[None, :] & (blk_id[None, :] > blk_id[:, None]) & (blk_e[None, :] != blk_e[:, None])
    nxt_blk = jnp.min(jnp.where(later_other, blk_id[None, :], nblk), axis=1)
    blk_next = jnp.sum(jnp.where(blk_id[None, :] == nxt_blk[:, None], blk_e[None, :] + 1, 0), axis=1) - 1
    meta = (blk_e, nact, blk_valid, blk_first, blk_group.astype(jnp.int32), blk_next.astype(jnp.int32),
            jnp.sum(blk_first).astype(jnp.int32).reshape(1))
    xs = jnp.zeros((nblk * tmb, d // 2), jnp.uint32)
    dests = []
    for (x1, h2), (top_i, top_w, rank) in zip(groups, routed):
        first = jnp.sum(jnp.where(top_i[None] == experts[:, None, None], pstart[:, None, None], 0), axis=0)
        dest = first + rank
        dests.append(dest)
        xs = _dispatch(h2, dest, xs, _tile(h2.shape[0], 256))
    ys = _moe_experts(xs, meta, layer, wg, bg, wu, bu, wd, bd, tmb, sub, MOE_UP_COLS, MOE_DOWN_COLS)
    outs = []
    for (x1, h2), (top_i, top_w, rank), dest in zip(groups, routed, dests):
        outs.append(_combine(x1, dest, top_w, ys, final_g, normalize, _tile(x1.shape[0], 256)))
    return outs


def kernel(x_prompt, x_sample, cache_k, cache_v, state_conv, page_table, norm1_g, w_in, w_dw, b_dw, conv_ln_g, conv_ln_b, w_conv_out, b_conv_out, lambda_q1, lambda_k1, lambda_q2, lambda_k2, subln_g, w_attn_out, w_o, norm2_g, w_router, b_router, w_exp_gate, b_exp_gate, w_exp_up, b_exp_up, w_exp_down, b_exp_down, final_norm_g):
    depth = w_in.shape[0]
    batch, seq, d = x_prompt.shape
    dec_b, dec_seq, _ = x_sample.shape
    assert dec_seq == 1
    past_len = page_table.shape[1] * cache_k.shape[2]
    xp = x_prompt.reshape(batch * seq, d)
    xs = x_sample.reshape(dec_b * dec_seq, d)
    outs = [[] for _ in range(6)]
    for l in range(depth):
        lam_init = 0.8 - 0.6 * math.exp(-0.3 * l)
        lams = (lambda_q1[l], lambda_k1[l], lambda_q2[l], lambda_k2[l])
        lw = (norm1_g[l], w_in[l].astype(BF16), w_dw[l], b_dw[l], conv_ln_g[l], conv_ln_b[l],
              w_conv_out[l].astype(BF16), b_conv_out[l], lams, subln_g[l], w_attn_out[l].astype(BF16),
              w_o[l].astype(BF16), norm2_g[l])
        x1p, h2p, kp, vp, cp = _token_mixers(xp, True, batch, seq, lw, lam_init, None)
        x1s, h2s, ks, vs, cs = _token_mixers(xs, False, dec_b, dec_seq, lw, lam_init,
                                             (state_conv[l], cache_k, cache_v, l, page_table, past_len))
        xp, xs = _moe([(x1p, h2p), (x1s, h2s)], w_router[l], b_router[l], l, w_exp_gate, b_exp_gate,
                      w_exp_up, b_exp_up, w_exp_down, b_exp_down, final_norm_g, l == depth - 1)
        for lst, v in zip(outs, (kp, vp, cp, ks, vs, cs)):
            lst.append(v)
    y_prompt = xp.reshape(batch, seq, d)
    y_sample = xs.reshape(dec_b, dec_seq, d)
    return (y_prompt, y_sample) + tuple(jnp.stack(v) for v in outs)
```

```python
import functools
import math

import jax
import jax.numpy as jnp
from jax import lax
from jax.experimental import pallas as pl
from jax.experimental.pallas import tpu as pltpu

RMS_EPS = 1e-5
LN_EPS = 1e-5
CONV_WIDTH = 31
CONV_HALO = 32
N_HEADS = 8
HEAD_DIM = 64
V_DIM = 128
HEAD_SHIFT = 6
TOP_K = 4
SWIGLU_ALPHA = 1.702
SWIGLU_LIMIT = 7.0
DECODE_PAGES_PER_STEP = 16
MOE_ROW_BLOCK = 512
MOE_ROW_SUBBLOCK = 256
MOE_UP_COLS = 1024
MOE_DOWN_COLS = 2048
NEG_BIG = -1e30

V7X_VMEM_LIMIT_BYTES = 56 * 1024 * 1024
V7X_LANES = 128
V7X_SUBLANES = 8
BF16 = jnp.bfloat16
F32 = jnp.float32


def _params(n_axes):
    return pltpu.CompilerParams(dimension_semantics=("arbitrary",) * n_axes,
                                vmem_limit_bytes=V7X_VMEM_LIMIT_BYTES)


def _rms(x, g):
    return x * lax.rsqrt(jnp.mean(x * x, axis=-1, keepdims=True) + RMS_EPS) * g


def _lam_value(lq1, lk1, lq2, lk2, lam_init):
    s1 = jnp.sum(lq1 * lk1, axis=-1, keepdims=True)
    s2 = jnp.sum(lq2 * lk2, axis=-1, keepdims=True)
    return jnp.exp(s1) - jnp.exp(s2) + lam_init


def _inproj_kernel(x_ref, g_ref, w_ref, z_ref, *rest, kt_tile):
    @pl.when(pl.program_id(1) == 0)
    def _():
        rest[-1][...] = _rms(x_ref[...], g_ref[...]).astype(BF16)

    z = jnp.dot(rest[-1][...], w_ref[...], preferred_element_type=F32)
    z_ref[...] = z
    if kt_tile is not None:
        @pl.when(pl.program_id(1) == kt_tile)
        def _():
            rest[0][...] = z.T


def _inproj(x, g, w_bf16, tm, tn, kt=None):
    t, d = x.shape
    n = w_bf16.shape[1]
    out_specs = [pl.BlockSpec((tm, tn), lambda i, j: (i, j))]
    out_shape = [jax.ShapeDtypeStruct((t, n), F32)]
    kt_tile = None
    if kt is not None:
        batch, seq, col_k = kt
        per_seq = seq // tm
        kt_tile = col_k // tn
        out_specs.append(pl.BlockSpec((None, tn, tm), lambda i, j: (i // per_seq, 0, i % per_seq)))
        out_shape.append(jax.ShapeDtypeStruct((batch, tn, seq), F32))
    return pl.pallas_call(
        functools.partial(_inproj_kernel, kt_tile=kt_tile),
        grid=(t // tm, n // tn),
        in_specs=[pl.BlockSpec((tm, d), lambda i, j: (i, 0)),
                  pl.BlockSpec((1, d), lambda i, j: (0, 0)),
                  pl.BlockSpec((d, tn), lambda i, j: (0, j))],
        out_specs=out_specs,
        out_shape=out_shape,
        scratch_shapes=[pltpu.VMEM((tm, d), BF16)],
        compiler_params=_params(2),
        name="inproj",
    )(x, g.reshape(1, d), w_bf16)


def _ln_silu(y, lng, lnb):
    mu = jnp.mean(y, axis=-1, keepdims=True)
    yc = y - mu
    var = jnp.mean(yc * yc, axis=-1, keepdims=True)
    yn = yc * lax.rsqrt(var + LN_EPS) * lng + lnb
    return yn * jax.nn.sigmoid(yn)


def _conv_prompt_kernel(a_ref, g_ref, wdw_ref, bdw_ref, lng_ref, lnb_ref, u_ref, c_ref, win_sc, shift_sc, *, rows):
    ts = a_ref.shape[0]
    sub = shift_sc.shape[0]

    @pl.when(pl.program_id(1) == 0)
    def _():
        win_sc[0:CONV_HALO, :] = jnp.zeros((CONV_HALO, win_sc.shape[1]), F32)

    u = a_ref[...] * jax.nn.sigmoid(g_ref[...])
    u_ref[...] = u
    win_sc[CONV_HALO:CONV_HALO + ts, :] = u
    span = shift_sc.shape[1]
    for r in range(1, sub):
        shift_sc[r] = win_sc[r:r + span, :]
    off = CONV_HALO - (CONV_WIDTH - 1)
    for r0 in range(0, ts, rows):
        acc = jnp.zeros((rows, u.shape[1]), F32)
        for j in range(CONV_WIDTH):
            whole, r = divmod(off + j, sub)
            lo = r0 + whole * sub
            tap = win_sc[lo:lo + rows, :] if r == 0 else shift_sc[r, lo:lo + rows, :]
            acc = acc + wdw_ref[j:j + 1, :] * tap
        y = acc + bdw_ref[...]
        c_ref[r0:r0 + rows, :] = _ln_silu(y, lng_ref[...], lnb_ref[...]).astype(BF16)
    win_sc[0:CONV_HALO, :] = win_sc[ts:ts + CONV_HALO, :]


def _conv_prompt(z, batch, seq, dc, w_dw, b_dw, ln_g, ln_b, ts):
    t = z.shape[0]
    nt = seq // ts
    vec = lambda v: v.reshape(1, dc)
    const = lambda shape: pl.BlockSpec(shape, lambda b, i: (0, 0))
    return pl.pallas_call(
        functools.partial(_conv_prompt_kernel, rows=32),
        grid=(batch, nt),
        in_specs=[pl.BlockSpec((ts, dc), lambda b, i: (b * nt + i, 0)),
                  pl.BlockSpec((ts, dc), lambda b, i: (b * nt + i, 1)),
                  const((CONV_WIDTH, dc)), const((1, dc)), const((1, dc)), const((1, dc))],
        out_specs=[pl.BlockSpec((ts, dc), lambda b, i: (b * nt + i, 0)),
                   pl.BlockSpec((ts, dc), lambda b, i: (b * nt + i, 0))],
        out_shape=[jax.ShapeDtypeStruct((t, dc), F32), jax.ShapeDtypeStruct((t, dc), BF16)],
        scratch_shapes=[pltpu.VMEM((ts + CONV_HALO, dc), F32),
                        pltpu.VMEM((V7X_SUBLANES, ts + CONV_HALO - V7X_SUBLANES, dc), F32)],
        compiler_params=_params(2),
        name="conv_prompt",
    )(z, z, w_dw, vec(b_dw), vec(ln_g), vec(ln_b))


def _conv_sample_kernel(a_ref, g_ref, st_ref, wdw_ref, bdw_ref, lng_ref, lnb_ref, u_ref, c_ref):
    nb = a_ref.shape[0]
    hist = CONV_WIDTH - 1
    u = a_ref[...] * jax.nn.sigmoid(g_ref[...])
    u_ref[...] = u

    def body(b, carry):
        past = jnp.sum(st_ref[b] * wdw_ref[0:hist, :], axis=0, keepdims=True)
        ub = u_ref[pl.ds(b, 1), :]
        y = past + ub * wdw_ref[hist:hist + 1, :] + bdw_ref[...]
        c_ref[pl.ds(b, 1), :] = _ln_silu(y, lng_ref[...], lnb_ref[...])
        return carry

    lax.fori_loop(0, nb, body, 0)


def _conv_sample(z, state, dc, w_dw, b_dw, ln_g, ln_b):
    nb = z.shape[0]
    vec = lambda v: v.reshape(1, dc)
    const = lambda shape: pl.BlockSpec(shape, lambda i: (0,) * len(shape))
    return pl.pallas_call(
        _conv_sample_kernel,
        grid=(1,),
        in_specs=[pl.BlockSpec((nb, dc), lambda i: (0, 0)),
                  pl.BlockSpec((nb, dc), lambda i: (0, 1)),
                  const(state.shape), const((CONV_WIDTH, dc)), const((1, dc)), const((1, dc)), const((1, dc))],
        out_specs=[const((nb, dc)), const((nb, dc))],
        out_shape=[jax.ShapeDtypeStruct((nb, dc), F32), jax.ShapeDtypeStruct((nb, dc), F32)],
        compiler_params=_params(1),
        name="conv_sample",
    )(z, z, state, w_dw, vec(b_dw), vec(ln_g), vec(ln_b))


def _attn_prompt_kernel(slopes_ref, q_ref, k_ref, v_ref, lq1_ref, lk1_ref, lq2_ref, lk2_ref, g_ref, o_ref,
                        vt_sc, kb_sc, acc_sc, *, lam_init):
    tq = q_ref.shape[0]
    n_kv = vt_sc.shape[0]
    h = pl.program_id(1)
    qi = pl.program_id(2)
    slope = slopes_ref[h]
    lam = _lam_value(lq1_ref[...], lk1_ref[...], lq2_ref[...], lk2_ref[...], lam_init)

    @pl.when(qi == 0)
    def _():
        for jb in range(n_kv):
            vt_sc[jb] = v_ref[jb * tq:(jb + 1) * tq, :].T.astype(BF16)
        kb_sc[...] = slope * lax.broadcasted_iota(jnp.int32, kb_sc.shape, 0).astype(F32)

    q = (q_ref[...] * (HEAD_DIM ** -0.5)).astype(BF16)
    lane = lax.broadcasted_iota(jnp.int32, q.shape, 1)
    zero = jnp.zeros_like(q)
    qq = jnp.concatenate([jnp.where(lane < HEAD_DIM, q, zero), jnp.where(lane >= HEAD_DIM, q, zero)], axis=0)
    acc_sc[...] = jnp.zeros(acc_sc.shape, F32)

    def step(j, carry, diagonal):
        m, l = carry
        kb = k_ref[pl.ds(pl.multiple_of(j * tq, tq), tq), :].astype(BF16)
        s = lax.dot_general(kb, qq, (((1,), (1,)), ((), ())), preferred_element_type=F32)
        bias = kb_sc[...]
        s = s + jnp.concatenate([bias] * (2 * tq // bias.shape[1]), axis=1)
        if diagonal:
            r = lax.broadcasted_iota(jnp.int32, s.shape, 0)
            c = lax.broadcasted_iota(jnp.int32, s.shape, 1)
            c = jnp.where(c >= tq, c - tq, c)
            s = jnp.where(r <= c, s, -jnp.inf)
        off = slope * ((j - qi) * tq).astype(F32)
        m_new = jnp.maximum(m, jnp.max(s, axis=0, keepdims=True) + off)
        alpha = jnp.exp(m - m_new)
        p = jnp.exp(s - (m_new - off))
        l = alpha * l + jnp.sum(p, axis=0, keepdims=True)
        acc_sc[...] = alpha * acc_sc[...] + jnp.dot(vt_sc[j], p.astype(BF16), preferred_element_type=F32)
        return m_new, l

    init = (jnp.full((1, 2 * tq), NEG_BIG, F32), jnp.zeros((1, 2 * tq), F32))
    carry = lax.fori_loop(0, qi, lambda j, c: step(j, c, False), init)
    _, l = step(qi, carry, True)
    o = acc_sc[...] / l
    d = o[:, :tq] - lam * o[:, tq:]
    dn = d * lax.rsqrt(jnp.mean(d * d, axis=0, keepdims=True) + RMS_EPS)
    o_ref[...] = (dn.T * g_ref[...] * (1.0 - lam_init)).astype(BF16)


def _attn_prompt(z, batch, seq, col_q, col_k, col_v, lams, subln_g, lam_init, tq):
    t = z.shape[0]
    nq = seq // tq
    slopes = 2.0 ** (-8.0 * jnp.arange(1, N_HEADS + 1, dtype=F32) / N_HEADS)
    blk = 2 * HEAD_DIM
    lam_spec = pl.BlockSpec((1, HEAD_DIM), lambda b, h, i, s: (0, 0))
    grid_spec = pltpu.PrefetchScalarGridSpec(
        num_scalar_prefetch=1,
        grid=(batch, N_HEADS, nq),
        in_specs=[pl.BlockSpec((tq, blk), lambda b, h, i, s: (b * nq + i, col_q // blk + h)),
                  pl.BlockSpec((seq, blk), lambda b, h, i, s: (b, col_k // blk + h)),
                  pl.BlockSpec((seq, V_DIM), lambda b, h, i, s: (b, col_v // V_DIM + h)),
                  lam_spec, lam_spec, lam_spec, lam_spec,
                  pl.BlockSpec((1, V_DIM), lambda b, h, i, s: (0, 0))],
        out_specs=pl.BlockSpec((tq, V_DIM), lambda b, h, i, s: (b * nq + i, h)),
        scratch_shapes=[pltpu.VMEM((nq, V_DIM, tq), BF16), pltpu.VMEM((tq, V7X_LANES), F32),
                        pltpu.VMEM((V_DIM, 2 * tq), F32)],
    )
    return pl.pallas_call(
        functools.partial(_attn_prompt_kernel, lam_init=lam_init),
        grid_spec=grid_spec,
        out_shape=jax.ShapeDtypeStruct((t, N_HEADS * V_DIM), BF16),
        compiler_params=_params(3),
        name="attn_prompt",
    )(slopes, z, z, z, *[v.reshape(1, HEAD_DIM) for v in lams], subln_g.reshape(1, V_DIM))


def _attn_decode_kernel(pt_ref, q_ref, kn_ref, vn_ref, spread_ref, *refs, past_len, pages):
    kc_refs, vc_refs = refs[:pages], refs[pages:2 * pages]
    o_ref, m_sc, l_sc, acc_sc, v_sc = refs[2 * pages:]
    p = pl.program_id(1)
    n_steps = pl.num_programs(1)
    n_maps = 2 * N_HEADS
    keys = kc_refs[0].shape[2]

    @pl.when(p == 0)
    def _():
        m_sc[...] = jnp.full(m_sc.shape, NEG_BIG, F32)
        l_sc[...] = jnp.zeros(l_sc.shape, F32)
        acc_sc[...] = jnp.zeros(acc_sc.shape, F32)

    q_row = q_ref[...] * (HEAD_DIM ** -0.5)
    qrow_i = lax.broadcasted_iota(jnp.int32, (n_maps, n_maps * HEAD_DIM), 0)
    qcol_i = lax.broadcasted_iota(jnp.int32, (n_maps, n_maps * HEAD_DIM), 1)
    qbd = jnp.where((qcol_i >> HEAD_SHIFT) == qrow_i, q_row, 0.0).astype(BF16)
    row1 = lax.broadcasted_iota(jnp.int32, (n_maps, 1), 0)
    slope = jnp.exp2(-((row1 >> 1) + 1).astype(F32))

    s = jnp.concatenate(
        [jnp.dot(qbd, kc_ref[...].reshape(n_maps * HEAD_DIM, keys).astype(BF16), preferred_element_type=F32)
         for kc_ref in kc_refs], axis=1)
    k_pos = p * (pages * keys) + lax.broadcasted_iota(jnp.int32, s.shape, 1)
    s = s - slope * (past_len - k_pos).astype(F32)

    m_prev = m_sc[...]
    m_new = jnp.maximum(m_prev, jnp.max(s, axis=1, keepdims=True))
    alpha = jnp.exp(m_prev - m_new)
    pe = jnp.exp(s - m_new)
    l_sc[...] = alpha * l_sc[...] + jnp.sum(pe, axis=1, keepdims=True)
    m_sc[...] = m_new

    pb = pe.astype(BF16)
    stacked = jnp.concatenate([pb[:, i * keys:(i + 1) * keys] for i in range(pages)], axis=0)
    spread = jnp.dot(stacked, spread_ref[...], preferred_element_type=F32)
    srow = lax.broadcasted_iota(jnp.int32, spread.shape, 0)
    scol = lax.broadcasted_iota(jnp.int32, spread.shape, 1)
    own_head = (scol & (N_HEADS - 1)) == ((srow & (n_maps - 1)) >> 1)
    pc = jnp.where(own_head, spread, 0.0).astype(BF16)
    pc = jnp.concatenate([pc[i * n_maps:(i + 1) * n_maps, :] for i in range(pages)], axis=1)
    rows = keys * N_HEADS
    for i, vc_ref in enumerate(vc_refs):
        v_sc[i * rows:(i + 1) * rows, :] = vc_ref[...].reshape(rows, V_DIM).astype(BF16)
    acc_sc[...] = alpha * acc_sc[...] + jnp.dot(pc, v_sc[...], preferred_element_type=F32)

    @pl.when(p == n_steps - 1)
    def _():
        kn = kn_ref[...].astype(BF16).astype(F32)
        s_new = jnp.sum(qbd.astype(F32) * kn, axis=1, keepdims=True)
        m_old = m_sc[...]
        m_fin = jnp.maximum(m_old, s_new)
        a = jnp.exp(m_old - m_fin)
        p_new = jnp.exp(s_new - m_fin)
        l_fin = a * l_sc[...] + p_new
        vn = vn_ref[...].astype(BF16).astype(F32)
        acc = a * acc_sc[...] + p_new.astype(BF16).astype(F32) * vn
        o_ref[...] = acc / l_fin


def _attn_decode(q_s, kn_s, vn_dup, cache_kt, cache_v, layer, page_table, past_len):
    nb, n_pages = page_table.shape
    n_maps = 2 * N_HEADS
    page = cache_kt.shape[-1]
    pages = DECODE_PAGES_PER_STEP if n_pages % DECODE_PAGES_PER_STEP == 0 else 1
    spread = (jnp.arange(page)[:, None] == jnp.arange(page * N_HEADS)[None, :] // N_HEADS).astype(BF16)

    def page_spec(shape, i):
        return pl.BlockSpec((None, None) + shape,
                            lambda b, p, pt: (layer, pt[b * n_pages + p * pages + i], 0, 0, 0))

    grid_spec = pltpu.PrefetchScalarGridSpec(
        num_scalar_prefetch=1,
        grid=(nb, n_pages // pages),
        in_specs=[pl.BlockSpec((None, 1, n_maps * HEAD_DIM), lambda b, p, pt: (b, 0, 0)),
                  pl.BlockSpec((None, 1, n_maps * HEAD_DIM), lambda b, p, pt: (b, 0, 0)),
                  pl.BlockSpec((None, n_maps, V_DIM), lambda b, p, pt: (b, 0, 0)),
                  pl.BlockSpec((page, page * N_HEADS), lambda b, p, pt: (0, 0))]
                 + [page_spec((n_maps, HEAD_DIM, page), i) for i in range(pages)]
                 + [page_spec((page, N_HEADS, V_DIM), i) for i in range(pages)],
        out_specs=pl.BlockSpec((None, n_maps, V_DIM), lambda b, p, pt: (b, 0, 0)),
        scratch_shapes=[pltpu.VMEM((n_maps, 1), F32), pltpu.VMEM((n_maps, 1), F32),
                        pltpu.VMEM((n_maps, V_DIM), F32), pltpu.VMEM((pages * page * N_HEADS, V_DIM), BF16)],
    )
    return pl.pallas_call(
        functools.partial(_attn_decode_kernel, past_len=past_len, pages=pages),
        grid_spec=grid_spec,
        out_shape=jax.ShapeDtypeStruct((nb, n_maps, V_DIM), F32),
        compiler_params=_params(2),
        name="attn_decode",
    )(page_table.reshape(-1), q_s, kn_s, vn_dup, spread, *([cache_kt] * pages), *([cache_v] * pages))


def _diff_norm_kernel(o_ref, lq1_ref, lk1_ref, lq2_ref, lk2_ref, g_ref, out_ref, *, lam_init):
    lam = _lam_value(lq1_ref[...], lk1_ref[...], lq2_ref[...], lk2_ref[...], lam_init)
    o = o_ref[...]
    d = o[:, :V_DIM] - lam * o[:, V_DIM:]
    out_ref[...] = _rms(d, g_ref[...]) * (1.0 - lam_init)


def _diff_norm(o_pairs, lams, subln_g, lam_init):
    rows = o_pairs.shape[0]
    const = lambda shape: pl.BlockSpec(shape, lambda i: (0, 0))
    return pl.pallas_call(
        functools.partial(_diff_norm_kernel, lam_init=lam_init),
        grid=(1,),
        in_specs=[const(o_pairs.shape)] + [const((1, HEAD_DIM))] * 4 + [const((1, V_DIM))],
        out_specs=const((rows, V_DIM)),
        out_shape=jax.ShapeDtypeStruct((rows, V_DIM), F32),
        compiler_params=_params(1),
        name="diff_norm",
    )(o_pairs, *[v.reshape(1, HEAD_DIM) for v in lams], subln_g.reshape(1, V_DIM))


def _pack_bf16_pairs(x):
    n = x.shape[1] // 2
    bits = lax.bitcast_convert_type(x.astype(BF16).astype(F32), jnp.uint32)
    return (bits[:, :n] >> 16) | (bits[:, n:] & jnp.uint32(0xFFFF0000))


def _unpack_bf16_pairs(w):
    lo = lax.bitcast_convert_type(w << 16, F32)
    hi = lax.bitcast_convert_type(w & jnp.uint32(0xFFFF0000), F32)
    return jnp.concatenate([lo, hi], axis=1).astype(BF16)


def _merge_kernel(x_ref, c_ref, o_ref, gc0_ref, gc1_ref, ga0_ref, ga1_ref, wco_ref, bco_ref, wao_ref, wo_ref,
                  g2_ref, x1_ref, h2_ref):
    half = gc0_ref.shape[1]
    conv_out = jnp.dot(c_ref[...].astype(BF16), wco_ref[...], preferred_element_type=F32) + bco_ref[...]
    attn_out = jnp.dot(o_ref[...].astype(BF16), wao_ref[...], preferred_element_type=F32)
    mixed = []
    for n, (gc_ref, ga_ref) in enumerate(((gc0_ref, ga0_ref), (gc1_ref, ga1_ref))):
        sl = slice(n * half, (n + 1) * half)
        mixed.append((jax.nn.sigmoid(gc_ref[...]) * conv_out[:, sl]
                      + jax.nn.sigmoid(ga_ref[...]) * attn_out[:, sl]).astype(BF16))
    x1 = x_ref[...] + jnp.dot(jnp.concatenate(mixed, axis=1), wo_ref[...], preferred_element_type=F32)
    x1_ref[...] = x1
    h2_ref[...] = _pack_bf16_pairs(_rms(x1, g2_ref[...]))


def _merge(x, c_act, o_norm, z, col_gc, col_ga, wco, bco, wao, wo, g2, tm):
    t, d = x.shape
    dc, da = c_act.shape[1], o_norm.shape[1]
    half = d // 2
    row = lambda w: pl.BlockSpec((tm, w), lambda i: (i, 0))
    gate = lambda blk: pl.BlockSpec((tm, half), lambda i: (i, blk))
    weight = lambda shape: pl.BlockSpec(shape, lambda i: (0, 0), pipeline_mode=pl.Buffered(1))
    return pl.pallas_call(
        _merge_kernel,
        grid=(t // tm,),
        in_specs=[row(d), row(dc), row(da),
                  gate(col_gc // half), gate(col_gc // half + 1), gate(col_ga // half), gate(col_ga // half + 1),
                  weight((dc, d)), weight((1, d)), weight((da, d)), weight((d, d)), weight((1, d))],
        out_specs=[row(d), row(half)],
        out_shape=[jax.ShapeDtypeStruct((t, d), F32), jax.ShapeDtypeStruct((t, half), jnp.uint32)],
        compiler_params=_params(1),
        name="merge",
    )(x, c_act, o_norm, z, z, z, z, wco, bco.reshape(1, d), wao, wo, g2.reshape(1, d))


def _router_kernel(h_ref, wr_ref, br_ref, base_ref, tri_ref, ti_ref, tw_ref, rk_ref, cnt_ref, cnt_sc, *, n_tokens):
    i = pl.program_id(0)
    tb = h_ref.shape[0]
    n_exp = wr_ref.shape[0]

    @pl.when(i == 0)
    def _():
        cnt_sc[...] = base_ref[...]

    logits = lax.dot_general(wr_ref[...], _unpack_bf16_pairs(h_ref[...]), (((1,), (1,)), ((), ())),
                             preferred_element_type=F32) + br_ref[...]
    tok = i * tb + lax.broadcasted_iota(jnp.int32, (1, tb), 1)
    valid = tok < n_tokens
    eidx = lax.broadcasted_iota(jnp.int32, (n_exp, tb), 0)
    run = cnt_sc[...]
    x = logits
    vals = []
    for k in range(TOP_K):
        mx = jnp.max(x, axis=0, keepdims=True)
        sel = jnp.min(jnp.where(x == mx, eidx, n_exp), axis=0, keepdims=True)
        hit = eidx == sel
        x = jnp.where(hit, -jnp.inf, x)
        onehot = jnp.where(hit & valid, 1.0, 0.0)
        before = jnp.dot(onehot.astype(BF16), tri_ref[...], preferred_element_type=F32)
        rank = jnp.sum(onehot * (before + run), axis=0, keepdims=True)
        run = run + jnp.sum(onehot, axis=1, keepdims=True)
        vals.append(mx)
        ti_ref[k:k + 1, :] = sel
        rk_ref[k:k + 1, :] = rank.astype(jnp.int32)
    es = [jnp.exp(v - vals[0]) for v in vals]
    tot = es[0] + es[1] + es[2] + es[3]
    for k in range(TOP_K):
        tw_ref[k:k + 1, :] = es[k] / tot
    cnt_sc[...] = run
    cnt_ref[...] = run


def _router(h2, wr_t, b_router, base, tb):
    t, d = h2.shape
    n_exp = wr_t.shape[0]
    nb = pl.cdiv(t, tb)
    tri = (jnp.arange(tb)[:, None] < jnp.arange(tb)[None, :]).astype(BF16)
    const = lambda shape: pl.BlockSpec(shape, lambda i: (0, 0))
    tok = lambda dt: jax.ShapeDtypeStruct((TOP_K, t), dt)
    return pl.pallas_call(
        functools.partial(_router_kernel, n_tokens=t),
        grid=(nb,),
        in_specs=[pl.BlockSpec((tb, d), lambda i: (i, 0)), const(wr_t.shape), const((n_exp, 1)),
                  const((n_exp, 1)), const((tb, tb))],
        out_specs=[pl.BlockSpec((TOP_K, tb), lambda i: (0, i))] * 3 + [const((n_exp, 1))],
        out_shape=[tok(jnp.int32), tok(F32), tok(jnp.int32), jax.ShapeDtypeStruct((n_exp, 1), F32)],
        scratch_shapes=[pltpu.VMEM((n_exp, 1), F32)],
        compiler_params=_params(1),
        name="router",
    )(h2, wr_t, b_router.reshape(n_exp, 1), base, tri)


def _block_index_list(dest, tb):
    t = dest.shape[1]
    return dest.reshape(TOP_K, t // tb, tb).transpose(1, 0, 2).reshape(-1)


def _dispatch_kernel(dest_ref, h_ref, xs_in_ref, xs_ref, idx_sm, idx_sem, row_sem):
    del xs_in_ref
    i = pl.program_id(0)
    tb = h_ref.shape[0]
    n_idx = TOP_K * tb
    cp = pltpu.make_async_copy(dest_ref.at[pl.ds(pl.multiple_of(i * n_idx, n_idx), n_idx)], idx_sm, idx_sem)
    cp.start()
    cp.wait()

    def row_copy(t, k):
        d = idx_sm[k * tb + t]
        return pltpu.make_async_copy(h_ref.at[pl.ds(t, 1), :], xs_ref.at[pl.ds(d, 1), :], row_sem)

    _run_row_copies(tb, row_copy)


def _run_row_copies(tb, row_copy):
    def issue(t, carry):
        for k in range(TOP_K):
            row_copy(t, k).start()
        return carry

    def drain(t, carry):
        for k in range(TOP_K):
            row_copy(t, k).wait()
        return carry

    lax.fori_loop(0, tb, issue, 0)
    lax.fori_loop(0, tb, drain, 0)


def _dispatch(h2, dest, xs, tb):
    t, d = h2.shape
    return pl.pallas_call(
        _dispatch_kernel,
        grid=(t // tb,),
        in_specs=[pl.BlockSpec(memory_space=pl.ANY),
                  pl.BlockSpec((tb, d), lambda i: (i, 0)),
                  pl.BlockSpec(memory_space=pl.ANY)],
        out_specs=pl.BlockSpec(memory_space=pl.ANY),
        out_shape=jax.ShapeDtypeStruct(xs.shape, xs.dtype),
        scratch_shapes=[pltpu.SMEM((TOP_K * tb,), jnp.int32), pltpu.SemaphoreType.DMA(()),
                        pltpu.SemaphoreType.DMA(())],
        input_output_aliases={2: 0},
        compiler_params=_params(1),
        name="moe_dispatch",
    )(_block_index_list(dest, tb), h2, xs)


def _combine_kernel(dest_ref, x1_ref, w_ref, g_ref, ys_ref, y_ref, idx_sm, rows_sc, idx_sem, row_sem, *, normalize):
    i = pl.program_id(0)
    tb = x1_ref.shape[0]
    n_idx = TOP_K * tb
    cp = pltpu.make_async_copy(dest_ref.at[pl.ds(pl.multiple_of(i * n_idx, n_idx), n_idx)], idx_sm, idx_sem)
    cp.start()
    cp.wait()

    def row_copy(t, k):
        d = idx_sm[k * tb + t]
        return pltpu.make_async_copy(ys_ref.at[pl.ds(d, 1), :], rows_sc.at[k, pl.ds(t, 1), :], row_sem)

    _run_row_copies(tb, row_copy)
    y = x1_ref[...]
    for k in range(TOP_K):
        y = y + w_ref[:, k:k + 1] * rows_sc[k]
    y_ref[...] = _rms(y, g_ref[...]) if normalize else y


def _combine(x1, dest, top_w, ys, final_g, normalize, tb):
    t, d = x1.shape
    return pl.pallas_call(
        functools.partial(_combine_kernel, normalize=normalize),
        grid=(t // tb,),
        in_specs=[pl.BlockSpec(memory_space=pl.ANY),
                  pl.BlockSpec((tb, d), lambda i: (i, 0)),
                  pl.BlockSpec((tb, TOP_K), lambda i: (i, 0)),
                  pl.BlockSpec((1, d), lambda i: (0, 0)),
                  pl.BlockSpec(memory_space=pl.ANY)],
        out_specs=pl.BlockSpec((tb, d), lambda i: (i, 0)),
        out_shape=jax.ShapeDtypeStruct((t, d), F32),
        scratch_shapes=[pltpu.SMEM((TOP_K * tb,), jnp.int32), pltpu.VMEM((TOP_K, tb, d), F32),
                        pltpu.SemaphoreType.DMA(()), pltpu.SemaphoreType.DMA(())],
        compiler_params=_params(1),
        name="moe_combine",
    )(_block_index_list(dest, tb), x1, top_w.T, final_g.reshape(1, d), ys)


def _block_rows(nact_ref, valid_ref, i):
    nact = nact_ref[0]
    ic = jnp.minimum(i, nact - 1)
    return ic, jnp.where(i < nact, valid_ref[ic], 0)


def _stream_expert_weights(meta, w_hbms, wbuf, w_scs, sems, layer, tn):
    blk_e_ref, nact_ref, valid_ref, first_ref, group_ref, next_ref, ngroups_ref = meta
    n = pl.program_id(0)
    i = pl.program_id(1)
    n_tiles = pl.num_programs(0)
    ic, n_valid = _block_rows(nact_ref, valid_ref, i)

    def copies(expert, slot, tile):
        return [pltpu.make_async_copy(w.at[layer, expert, :, pl.ds(pl.multiple_of(tile * tn, tn), tn)],
                                      wbuf.at[slot, k], sems.at[slot, k]) for k, w in enumerate(w_hbms)]

    @pl.when((n_valid > 0) & (first_ref[ic] == 1))
    def _():
        slot = (group_ref[ic] + n * ngroups_ref[0]) & 1
        mine = copies(blk_e_ref[ic], slot, n)

        @pl.when((i == 0) & (n == 0))
        def _():
            for c in mine:
                c.start()

        for c in mine:
            c.wait()
        nxt = next_ref[ic]

        @pl.when(nxt >= 0)
        def _():
            for c in copies(nxt, 1 - slot, n):
                c.start()

        @pl.when((nxt < 0) & (n + 1 < n_tiles))
        def _():
            for c in copies(blk_e_ref[0], 1 - slot, n + 1):
                c.start()

        for k, w_sc in enumerate(w_scs):
            w_sc[...] = wbuf[slot, k].astype(BF16)

    return n_valid


def _moe_gu_kernel(*refs, sub, layer, tn):
    meta = refs[:7]
    x_ref, wg_hbm, wu_hbm, bg_ref, bu_ref, h_ref, wbuf, wg_sc, wu_sc, sems = refs[7:]
    n_valid = _stream_expert_weights(meta, (wg_hbm, wu_hbm), wbuf, (wg_sc, wu_sc), sems, layer, tn)

    for r0 in range(0, x_ref.shape[0], sub):
        @pl.when(n_valid > r0)
        def _(r0=r0):
            x = _unpack_bf16_pairs(x_ref[r0:r0 + sub, :])
            gt = jnp.minimum(jnp.dot(x, wg_sc[...], preferred_element_type=F32) + bg_ref[...], SWIGLU_LIMIT)
            up = jnp.clip(jnp.dot(x, wu_sc[...], preferred_element_type=F32) + bu_ref[...],
                          -SWIGLU_LIMIT, SWIGLU_LIMIT)
            h_ref[r0:r0 + sub, :] = (gt * jax.nn.sigmoid(SWIGLU_ALPHA * gt) * (up + 1.0)).astype(BF16)

        @pl.when(n_valid <= r0)
        def _(r0=r0):
            h_ref[r0:r0 + sub, :] = jnp.zeros((sub, h_ref.shape[1]), h_ref.dtype)


def _moe_down_kernel(*refs, sub, layer, tn):
    meta = refs[:7]
    h_ref, wd_hbm, bd_ref, y_ref, wbuf, wd_sc, sems = refs[7:]
    n_valid = _stream_expert_weights(meta, (wd_hbm,), wbuf, (wd_sc,), sems, layer, tn)

    for r0 in range(0, h_ref.shape[0], sub):
        @pl.when(n_valid > r0)
        def _(r0=r0):
            y_ref[r0:r0 + sub, :] = (jnp.dot(h_ref[r0:r0 + sub, :], wd_sc[...], preferred_element_type=F32)
                                     + bd_ref[...])

        @pl.when(n_valid <= r0)
        def _(r0=r0):
            y_ref[r0:r0 + sub, :] = jnp.zeros((sub, y_ref.shape[1]), y_ref.dtype)


def _moe_experts(xs, meta, layer, wg, bg, wu, bu, wd, bd, tmb, sub, tn_up, tn_down):
    rows, d_packed = xs.shape
    d = 2 * d_packed
    _, n_exp, _, de = wg.shape
    nblk = rows // tmb
    clamp = lambda i, na: jnp.minimum(i, na[0] - 1)
    lhs = lambda w: pl.BlockSpec((tmb, w), lambda n, i, be, na, *_: (clamp(i, na), 0))
    bspec = lambda tn: pl.BlockSpec((None, 1, tn), lambda n, i, be, na, *_: (be[clamp(i, na)], 0, n))
    out = lambda tn: pl.BlockSpec((tmb, tn), lambda n, i, *_: (i, n))
    hbm = pl.BlockSpec(memory_space=pl.ANY)
    bg, bu, bd = bg[layer], bu[layer], bd[layer]
    hidden = pl.pallas_call(
        functools.partial(_moe_gu_kernel, sub=sub, layer=layer, tn=tn_up),
        grid_spec=pltpu.PrefetchScalarGridSpec(
            num_scalar_prefetch=len(meta), grid=(de // tn_up, nblk),
            in_specs=[lhs(d_packed), hbm, hbm, bspec(tn_up), bspec(tn_up)], out_specs=out(tn_up),
            scratch_shapes=[pltpu.VMEM((2, 2, d, tn_up), F32), pltpu.VMEM((d, tn_up), BF16),
                            pltpu.VMEM((d, tn_up), BF16), pltpu.SemaphoreType.DMA((2, 2))]),
        out_shape=jax.ShapeDtypeStruct((rows, de), BF16),
        compiler_params=_params(2),
        name="moe_gate_up",
    )(*meta, xs, wg, wu, bg.reshape(n_exp, 1, de), bu.reshape(n_exp, 1, de))
    return pl.pallas_call(
        functools.partial(_moe_down_kernel, sub=sub, layer=layer, tn=tn_down),
        grid_spec=pltpu.PrefetchScalarGridSpec(
            num_scalar_prefetch=len(meta), grid=(d // tn_down, nblk),
            in_specs=[lhs(de), hbm, bspec(tn_down)], out_specs=out(tn_down),
            scratch_shapes=[pltpu.VMEM((2, 1, de, tn_down), F32), pltpu.VMEM((de, tn_down), BF16),
                            pltpu.SemaphoreType.DMA((2, 1))]),
        out_shape=jax.ShapeDtypeStruct((rows, d), F32),
        compiler_params=_params(2),
        name="moe_down",
    )(*meta, hidden, wd, bd.reshape(n_exp, 1, d))


def _tile(n, pref):
    return pref if n % pref == 0 else n


def _token_mixers(x, is_prompt, batch, seq, lw, lam_init, sample_state):
    (norm1_g, w_in, w_dw, b_dw, conv_ln_g, conv_ln_b, w_conv_out, b_conv_out, lams, subln_g, w_attn_out, w_o,
     norm2_g) = lw
    t, d = x.shape
    dc = w_dw.shape[1]
    da = N_HEADS * V_DIM
    col_q, col_k, col_v = 2 * dc, 2 * dc + da, 2 * dc + 2 * da
    col_gc, col_ga = 2 * dc + 3 * da, 2 * dc + 3 * da + d
    v_new_of = lambda z: z[:, col_v:col_gc].reshape(batch, seq, N_HEADS, V_DIM)
    if is_prompt:
        z, kt = _inproj(x, norm1_g, w_in, _tile(seq, 1024), da, kt=(batch, seq, col_k))
        k_new = jnp.transpose(kt.reshape(batch, 2 * N_HEADS, HEAD_DIM, seq), (0, 3, 1, 2))
        v_new = v_new_of(z)
        u, c_act = _conv_prompt(z, batch, seq, dc, w_dw, b_dw, conv_ln_g, conv_ln_b, _tile(seq, 256))
        conv_new = u.reshape(batch, seq, dc)[:, seq - (CONV_WIDTH - 1):]
        o_norm = _attn_prompt(z, batch, seq, col_q, col_k, col_v, lams, subln_g, lam_init, _tile(seq, 512))
    else:
        state_conv, cache_k, cache_v, layer, page_table, past_len = sample_state
        z, = _inproj(x, norm1_g, w_in, t, 1536)
        k_new = z[:, col_k:col_v].reshape(batch, seq, 2 * N_HEADS, HEAD_DIM)
        v_new = v_new_of(z)
        u, c_act = _conv_sample(z, state_conv, dc, w_dw, b_dw, conv_ln_g, conv_ln_b)
        conv_new = jnp.concatenate([state_conv[:, 1:], u[:, None, :]], axis=1)
        vn_dup = jnp.repeat(v_new.reshape(t, N_HEADS, V_DIM), 2, axis=1)
        cache_kt = jnp.transpose(cache_k, (0, 1, 3, 4, 2))
        o_maps = _attn_decode(z[:, col_q:col_k].reshape(t, 1, da), z[:, col_k:col_v].reshape(t, 1, da), vn_dup,
                              cache_kt, cache_v, layer, page_table, past_len)
        o_norm = _diff_norm(o_maps.reshape(t * N_HEADS, 2 * V_DIM), lams, subln_g, lam_init).reshape(t, da)
    x1, h2 = _merge(x, c_act, o_norm, z, col_gc, col_ga, w_conv_out, b_conv_out, w_attn_out, w_o, norm2_g,
                    _tile(t, 256))
    return x1, h2, k_new, v_new, conv_new


def _moe(groups, w_router, b_router, layer, wg, bg, wu, bu, wd, bd, final_g, normalize):
    _, n_exp, d, _ = wg.shape
    tmb, sub = MOE_ROW_BLOCK, MOE_ROW_SUBBLOCK
    wr_t = w_router.T.astype(BF16)
    base = jnp.zeros((n_exp, 1), F32)
    routed = []
    for x1, h2 in groups:
        top_i, top_w, rank, base = _router(h2, wr_t, b_router, base, _tile(h2.shape[0], 512))
        routed.append((top_i, top_w, rank))
    counts = base[:, 0].astype(jnp.int32)
    padded = (counts + tmb - 1) // tmb * tmb
    pend = jnp.cumsum(padded)
    pstart = pend - padded
    n_assign = sum(h2.shape[0] for _, h2 in groups) * TOP_K
    nblk = pl.cdiv(n_assign, tmb) + n_exp
    nact = (pend[-1] // tmb).astype(jnp.int32).reshape(1)
    blk_row = jnp.arange(nblk, dtype=jnp.int32) * tmb
    blk_e = jnp.minimum(jnp.sum(pend[None, :] <= blk_row[:, None], axis=1), n_exp - 1).astype(jnp.int32)
    experts = jnp.arange(n_exp, dtype=jnp.int32)
    blk_last = jnp.sum(jnp.where(blk_e[:, None] == experts[None, :], (pstart + counts)[None, :], 0), axis=1)
    blk_valid = jnp.clip(blk_last - blk_row, 0, tmb).astype(jnp.int32)
    blk_id = jnp.arange(nblk, dtype=jnp.int32)
    active = blk_id < nact[0]
    blk_first = (active & (blk_e != jnp.concatenate([jnp.full((1,), -1, jnp.int32), blk_e[:-1]]))).astype(jnp.int32)
    blk_group = jnp.cumsum(blk_first) - 1
    later_other = active[None, :] & (blk_id[None, :] > blk_id[:, None]) & (blk_e[None, :] != blk_e[:, None])
    nxt_blk = jnp.min(jnp.where(later_other, blk_id[None, :], nblk), axis=1)
    blk_next = jnp.sum(jnp.where(blk_id[None, :] == nxt_blk[:, None], blk_e[None, :] + 1, 0), axis=1) - 1
    meta = (blk_e, nact, blk_valid, blk_first, blk_group.astype(jnp.int32), blk_next.astype(jnp.int32),
            jnp.sum(blk_first).astype(jnp.int32).reshape(1))
    xs = jnp.zeros((nblk * tmb, d // 2), jnp.uint32)
    dests = []
    for (x1, h2), (top_i, top_w, rank) in zip(groups, routed):
        first = jnp.sum(jnp.where(top_i[None] == experts[:, None, None], pstart[:, None, None], 0), axis=0)
        dest = first + rank
        dests.append(dest)
        xs = _dispatch(h2, dest, xs, _tile(h2.shape[0], 256))
    ys = _moe_experts(xs, meta, layer, wg, bg, wu, bu, wd, bd, tmb, sub, MOE_UP_COLS, MOE_DOWN_COLS)
    outs = []
    for (x1, h2), (top_i, top_w, rank), dest in zip(groups, routed, dests):
        outs.append(_combine(x1, dest, top_w, ys, final_g, normalize, _tile(x1.shape[0], 256)))
    return outs


def kernel(x_prompt, x_sample, cache_k, cache_v, state_conv, page_table, norm1_g, w_in, w_dw, b_dw, conv_ln_g, conv_ln_b, w_conv_out, b_conv_out, lambda_q1, lambda_k1, lambda_q2, lambda_k2, subln_g, w_attn_out, w_o, norm2_g, w_router, b_router, w_exp_gate, b_exp_gate, w_exp_up, b_exp_up, w_exp_down, b_exp_down, final_norm_g):
    depth = w_in.shape[0]
    batch, seq, d = x_prompt.shape
    dec_b, dec_seq, _ = x_sample.shape
    assert dec_seq == 1
    past_len = page_table.shape[1] * cache_k.shape[2]
    xp = x_prompt.reshape(batch * seq, d)
    xs = x_sample.reshape(dec_b * dec_seq, d)
    outs = [[] for _ in range(6)]
    for l in range(depth):
        lam_init = 0.8 - 0.6 * math.exp(-0.3 * l)
        lams = (lambda_q1[l], lambda_k1[l], lambda_q2[l], lambda_k2[l])
        lw = (norm1_g[l], w_in[l].astype(BF16), w_dw[l], b_dw[l], conv_ln_g[l], conv_ln_b[l],
              w_conv_out[l].astype(BF16), b_conv_out[l], lams, subln_g[l], w_attn_out[l].astype(BF16),
              w_o[l].astype(BF16), norm2_g[l])
        x1p, h2p, kp, vp, cp = _token_mixers(xp, True, batch, seq, lw, lam_init, None)
        x1s, h2s, ks, vs, cs = _token_mixers(xs, False, dec_b, dec_seq, lw, lam_init,
                                             (state_conv[l], cache_k, cache_v, l, page_table, past_len))
        xp, xs = _moe([(x1p, h2p), (x1s, h2s)], w_router[l], b_router[l], l, w_exp_gate, b_exp_gate,
                      w_exp_up, b_exp_up, w_exp_down, b_exp_down, final_norm_g, l == depth - 1)
        for lst, v in zip(outs, (kp, vp, cp, ks, vs, cs)):
            lst.append(v)
    y_prompt = xp.reshape(batch, seq, d)
    y_sample = xs.reshape(dec_b, dec_seq, d)
    return (y_prompt, y_sample) + tuple(jnp.stack(v) for v in outs)
```
